```python
import math
import jax
import jax.numpy as jnp
from jax import lax
import numpy as np

D_MODEL = 4096
BATCH = 4
SEQ = 2048
DEPTH = 1
DEC_BATCH = 32
DEC_SEQ = 1
PAST_LEN = 8192
PAGE_SIZE = 128

HEAD_DIM = 128
LRU_WIDTH = 1536
LRU_BLOCKS = 12
LRU_BLOCK = LRU_WIDTH // LRU_BLOCKS
CONV_WIDTH = 4
LRU_C = 8.0
DIL_PAIRS = ((128, 1), (512, 4), (2048, 16))
N_GROUPS = len(DIL_PAIRS)
HPG = 4
DIL_HEADS = N_GROUPS * HPG
DIL_WIDTH = DIL_HEADS * HEAD_DIM
DIL_OUT = HPG * HEAD_DIM
BAND = 128
DIL_SCALE = HEAD_DIM ** -0.5
N_BUCKETS = 32
MAX_DISTANCE = 2048
MEM_TOKENS = 256
MEM_HEADS = 4
MEM_HEAD_DIM = 256
MEM_WIDTH = MEM_HEADS * MEM_HEAD_DIM
MEM_SCALE = MEM_HEAD_DIM ** -0.5
N_BRANCH = 3
PEER_HEADS = 8
PEER_NKEYS = 128
PEER_N = PEER_NKEYS * PEER_NKEYS
PEER_DQ = 256
PEER_TOPK = 16
PEER_BLOCK = 128
EPS = 1e-6

O_LRU_X = 0
O_LRU_G = O_LRU_X + LRU_WIDTH
O_Q = O_LRU_G + LRU_WIDTH
O_K = O_Q + DIL_WIDTH
O_V = O_K + DIL_WIDTH
O_MQ = O_V + DIL_WIDTH
O_GATE = O_MQ + MEM_WIDTH
IN_COLS = O_GATE + N_BRANCH * D_MODEL

kernel_name = 'hawk_dilated_peer_hybrid_step'


def rmsnorm(x, g):
    xf = x.astype(jnp.float32)
    y = xf * lax.rsqrt(jnp.mean(xf * xf, axis=-1, keepdims=True) + EPS)
    return (y * g.astype(jnp.float32)).astype(x.dtype)


def t5_bucket(dist):
    max_exact = N_BUCKETS // 2
    d = jnp.maximum(dist, 1).astype(jnp.float32)
    large = max_exact + (jnp.log(d / max_exact) / math.log(MAX_DISTANCE / max_exact)
                         * (N_BUCKETS - max_exact)).astype(jnp.int32)
    large = jnp.minimum(large, N_BUCKETS - 1)
    return jnp.where(dist < max_exact, dist, large)


def group_bias(rel_bias, g):
    dil = DIL_PAIRS[g][1]
    dist = jnp.arange(BAND + 1, dtype=jnp.int32) * dil
    return rel_bias[t5_bucket(dist)][:, g * HPG:(g + 1) * HPG].astype(jnp.float32)


def rglru(x_in, gate_in, conv_state, h0, lp):
    B, T, _ = x_in.shape
    xp = jnp.concatenate([conv_state.astype(x_in.dtype), x_in], axis=1)
    xc = lp['conv_b'] + xp[:, 0:T] * lp['conv_w'][0]
    for k in range(1, CONV_WIDTH):
        xc = xc + xp[:, k:k + T] * lp['conv_w'][k]
    xb = xc.reshape(B, T, LRU_BLOCKS, LRU_BLOCK)
    r = jax.nn.sigmoid(jnp.einsum('btni,nio->btno', xb, lp['lru_wa']) + lp['lru_ba']).reshape(B, T, LRU_WIDTH)
    i = jax.nn.sigmoid(jnp.einsum('btni,nio->btno', xb, lp['lru_wi']) + lp['lru_bi']).reshape(B, T, LRU_WIDTH)
    log_a = -LRU_C * r.astype(jnp.float32) * jax.nn.softplus(-lp['lru_lambda'].astype(jnp.float32))
    a = jnp.exp(log_a)
    b = jnp.sqrt(-jnp.expm1(2.0 * log_a)) * (i * xc).astype(jnp.float32)
    b = b.at[:, 0].add(a[:, 0] * h0.astype(jnp.float32))

    def combine(lhs, rhs):
        a_l, b_l = lhs
        a_r, b_r = rhs
        return a_l * a_r, a_r * b_l + b_r

    _, h = lax.associative_scan(combine, (a, b), axis=1)
    y = h.astype(x_in.dtype) * jax.nn.gelu(gate_in)
    return y, xp[:, T:], h[:, -1].astype(x_in.dtype)


def dil_qkv(z, lp):
    B, T, _ = z.shape
    q = rmsnorm(z[..., O_Q:O_K].reshape(B, T, DIL_HEADS, HEAD_DIM), lp['dil_qn_g'])
    k = rmsnorm(z[..., O_K:O_V].reshape(B, T, DIL_HEADS, HEAD_DIM), lp['dil_kn_g'])
    v = z[..., O_V:O_MQ].reshape(B, T, DIL_HEADS, HEAD_DIM)
    return q, k, v


def dilated_prompt(q, k, v, bias_j, dil):
    B, S, H, Dh = q.shape
    L = S // dil
    nb = -(-L // BAND)
    Lp = nb * BAND

    def to_blocks(t):
        t = t.reshape(B, L, dil, H, Dh).transpose(0, 2, 1, 3, 4)
        t = jnp.pad(t, ((0, 0), (0, 0), (0, Lp - L), (0, 0), (0, 0)))
        return t.reshape(B, dil, nb, BAND, H, Dh)

    def with_prev(t):
        prev = jnp.pad(t[:, :, :-1], ((0, 0), (0, 0), (1, 0), (0, 0), (0, 0), (0, 0)))
        return jnp.concatenate([prev, t], axis=3)

    qb = to_blocks(q)
    kb = with_prev(to_blocks(k))
    vb = with_prev(to_blocks(v))
    qi = jnp.arange(BAND)[:, None]
    ki = jnp.arange(2 * BAND)[None, :]
    off = qi + BAND - ki
    key_pos = (jnp.arange(nb)[:, None] - 1) * BAND + jnp.arange(2 * BAND)[None, :]
    mask = ((off >= 0) & (off <= BAND))[None] & (key_pos >= 0)[:, None, :]
    bias = bias_j[jnp.clip(off, 0, BAND)].transpose(2, 0, 1)
    s = jnp.einsum('brnqhd,brnkhd->brnhqk', qb, kb).astype(jnp.float32) * DIL_SCALE + bias
    s = jnp.where(mask[:, None], s, -jnp.inf)
    m = jnp.max(s, axis=-1, keepdims=True)
    p = jnp.exp(s - m)
    denom = jnp.sum(p, axis=-1)
    o = jnp.einsum('brnhqk,brnkhd->brnqhd', p.astype(v.dtype), vb).astype(jnp.float32)
    o = o / jnp.moveaxis(denom, 3, 4)[..., None]
    lse = jnp.moveaxis(m[..., 0] + jnp.log(denom), 3, 4)
    o = o.reshape(B, dil, Lp, H, Dh)[:, :, :L].transpose(0, 2, 1, 3, 4).reshape(B, S, H, Dh)
    lse = lse.reshape(B, dil, Lp, H)[:, :, :L].transpose(0, 2, 1, 3).reshape(B, S, H)
    return o, lse


def dilated_sample(q, k_new, v_new, buf, bias_j, dil):
    T = q.shape[1]
    Wb = buf.shape[1]
    kk = jnp.concatenate([buf[:, :, 0].astype(k_new.dtype), k_new], axis=1)
    vv = jnp.concatenate([buf[:, :, 1].astype(v_new.dtype), v_new], axis=1)
    idx = Wb + jnp.arange(T)[:, None] - jnp.arange(BAND + 1)[None, :] * dil
    valid = idx >= 0
    idx = jnp.maximum(idx, 0)
    kg = kk[:, idx]
    vg = vv[:, idx]
    s = jnp.einsum('bthd,btjhd->bthj', q, kg).astype(jnp.float32) * DIL_SCALE + bias_j.T
    s = jnp.where(valid[None, :, None, :], s, -jnp.inf)
    m = jnp.max(s, axis=-1, keepdims=True)
    p = jnp.exp(s - m)
    denom = jnp.sum(p, axis=-1)
    o = jnp.einsum('bthj,btjhd->bthd', p.astype(vv.dtype), vg).astype(jnp.float32) / denom[..., None]
    lse = m[..., 0] + jnp.log(denom)
    new_buf = jnp.stack([kk, vv], axis=2)[:, T:]
    return o, lse, new_buf


def combine_groups(outs, lses, dtype):
    w = jax.nn.softmax(jnp.stack(lses, axis=0), axis=0)
    o = jnp.einsum('gbth,gbthd->bthd', w, jnp.stack(outs, axis=0))
    B, T = o.shape[:2]
    return o.reshape(B, T, DIL_OUT).astype(dtype)


def mem_kv(mem, lp):
    B, M, _ = mem.shape
    kv = (rmsnorm(mem, lp['mem_norm_g']) @ lp['w_mem_kv']).reshape(B, M, 2, MEM_HEADS, MEM_HEAD_DIM)
    k = rmsnorm(kv[:, :, 0], lp['mem_kn_g'])
    return jnp.stack([k, kv[:, :, 1]], axis=2)


def mem_attend(z, kv, lp):
    B, T, _ = z.shape
    q = rmsnorm(z[..., O_MQ:O_GATE].reshape(B, T, MEM_HEADS, MEM_HEAD_DIM), lp['mem_qn_g'])
    s = jnp.einsum('bthd,bmhd->bhtm', q, kv[:, :, 0]).astype(jnp.float32) * MEM_SCALE
    p = jax.nn.softmax(s, axis=-1)
    o = jnp.einsum('bhtm,bmhd->bthd', p.astype(kv.dtype), kv[:, :, 1])
    return o.reshape(B, T, MEM_WIDTH)


def merge_out(x, z, y_a, y_b, y_m, lp):
    B, T, _ = x.shape
    g = jax.nn.sigmoid(z[..., O_GATE:].reshape(B, T, N_BRANCH, D_MODEL))
    merged = (g[:, :, 0] * (y_a @ lp['w_br_a'])
              + g[:, :, 1] * (y_b @ lp['w_br_b'])
              + g[:, :, 2] * (y_m @ lp['w_br_m']))
    return x + merged @ lp['w_out']


def peer(x, lp):
    shp = x.shape
    h = rmsnorm(x, lp['norm2_g']).reshape(-1, D_MODEL)
    n = h.shape[0]
    nb = -(-n // PEER_BLOCK)
    h = jnp.pad(h, ((0, nb * PEER_BLOCK - n), (0, 0))).reshape(nb, PEER_BLOCK, D_MODEL)

    def block(xb):
        q = (xb @ lp['peer_wq']).reshape(PEER_BLOCK, PEER_HEADS, 2, PEER_DQ // 2)
        s = jnp.einsum('thpk,hpnk->thpn', q, lp['peer_subkeys']).astype(jnp.float32)
        s1, i1 = lax.top_k(s[:, :, 0], PEER_TOPK)
        s2, i2 = lax.top_k(s[:, :, 1], PEER_TOPK)
        cand = (s1[..., :, None] + s2[..., None, :]).reshape(PEER_BLOCK, PEER_HEADS, PEER_TOPK * PEER_TOPK)
        cand_id = (i1[..., :, None] * PEER_NKEYS + i2[..., None, :]).reshape(PEER_BLOCK, PEER_HEADS, PEER_TOPK * PEER_TOPK)
        st, pos = lax.top_k(cand, PEER_TOPK)
        ids = jnp.take_along_axis(cand_id, pos, axis=-1)
        g = jax.nn.softmax(st, axis=-1)
        act = jax.nn.gelu(jnp.einsum('td,thkd->thk', xb, lp['peer_u'][ids]).astype(jnp.float32))
        return jnp.einsum('thk,thkd->td', (g * act).astype(xb.dtype), lp['peer_v'][ids])

    out = lax.map(block, h).reshape(-1, D_MODEL)[:n].reshape(shp)
    return x + out


def layer_prompt(x, mem, rel_bias, lp):
    B, T, _ = x.shape
    z = rmsnorm(x, lp['norm1_g']) @ lp['w_in']
    y_a, conv_new, h_new = rglru(z[..., O_LRU_X:O_LRU_G], z[..., O_LRU_G:O_Q],
                                 jnp.zeros((B, CONV_WIDTH - 1, LRU_WIDTH), x.dtype),
                                 jnp.zeros((B, LRU_WIDTH), x.dtype), lp)
    q, k, v = dil_qkv(z, lp)
    outs, lses, bufs = [], [], []
    for g in range(N_GROUPS):
        w, d = DIL_PAIRS[g]
        hs = slice(g * HPG, (g + 1) * HPG)
        o, lse = dilated_prompt(q[:, :, hs], k[:, :, hs], v[:, :, hs], group_bias(rel_bias, g), d)
        outs.append(o)
        lses.append(lse)
        bufs.append(jnp.stack([k[:, :, hs], v[:, :, hs]], axis=2)[:, T - min(w, T):])
    y_b = combine_groups(outs, lses, x.dtype)
    mkv = mem_kv(mem, lp)
    y_m = mem_attend(z, mkv, lp)
    x = merge_out(x, z, y_a, y_b, y_m, lp)
    x = peer(x, lp)
    return x, conv_new, h_new, bufs, mkv


def layer_sample(x, conv_state, h0, bufs, mem_cache, rel_bias, lp):
    z = rmsnorm(x, lp['norm1_g']) @ lp['w_in']
    y_a, conv_new, h_new = rglru(z[..., O_LRU_X:O_LRU_G], z[..., O_LRU_G:O_Q], conv_state, h0, lp)
    q, k, v = dil_qkv(z, lp)
    outs, lses, new_bufs = [], [], []
    for g in range(N_GROUPS):
        hs = slice(g * HPG, (g + 1) * HPG)
        o, lse, nbuf = dilated_sample(q[:, :, hs], k[:, :, hs], v[:, :, hs], bufs[g],
                                      group_bias(rel_bias, g), DIL_PAIRS[g][1])
        outs.append(o)
        lses.append(lse)
        new_bufs.append(nbuf)
    y_b = combine_groups(outs, lses, x.dtype)
    y_m = mem_attend(z, mem_cache.astype(x.dtype), lp)
    x = merge_out(x, z, y_a, y_b, y_m, lp)
    x = peer(x, lp)
    return x, conv_new, h_new, new_bufs


def setup_inputs(seed: int = 0) -> dict:
    key = jax.random.key(seed)
    ks = jax.random.split(key, 40)

    def nrm(i, shape, scale):
        return jax.random.normal(ks[i], shape, jnp.float32) * scale

    a0 = jax.random.uniform(ks[39], (DEPTH, LRU_WIDTH), jnp.float32, 0.9, 0.999)
    root = a0 ** (1.0 / LRU_C)
    lam = jnp.log(root) - jnp.log1p(-root)
    win = [min(w, PAST_LEN) for w, _ in DIL_PAIRS]
    return {
        'x_prompt': nrm(0, (BATCH, SEQ, D_MODEL), 1.0),
        'x_sample': nrm(1, (DEC_BATCH, DEC_SEQ, D_MODEL), 1.0),
        'state_conv': nrm(2, (DEPTH, DEC_BATCH, CONV_WIDTH - 1, LRU_WIDTH), 0.5),
        'state_rnn': nrm(3, (DEPTH, DEC_BATCH, LRU_WIDTH), 0.5),
        'cache_swa0': nrm(4, (DEPTH, DEC_BATCH, win[0], 2, HPG, HEAD_DIM), 1.0),
        'cache_swa1': nrm(5, (DEPTH, DEC_BATCH, win[1], 2, HPG, HEAD_DIM), 1.0),
        'cache_swa2': nrm(6, (DEPTH, DEC_BATCH, win[2], 2, HPG, HEAD_DIM), 1.0),
        'cache_mem': nrm(7, (DEPTH, DEC_BATCH, MEM_TOKENS, 2, MEM_HEADS, MEM_HEAD_DIM), 1.0),
        'mem_prompt': nrm(8, (BATCH, MEM_TOKENS, D_MODEL), 1.0),
        'rel_bias': nrm(9, (N_BUCKETS, DIL_HEADS), 0.5),
        'norm1_g': 1.0 + nrm(10, (DEPTH, D_MODEL), 0.05),
        'w_in': nrm(11, (DEPTH, D_MODEL, IN_COLS), D_MODEL ** -0.5),
        'conv_w': nrm(12, (DEPTH, CONV_WIDTH, LRU_WIDTH), 0.5),
        'conv_b': nrm(13, (DEPTH, LRU_WIDTH), 0.02),
        'lru_wa': nrm(14, (DEPTH, LRU_BLOCKS, LRU_BLOCK, LRU_BLOCK), LRU_BLOCK ** -0.5),
        'lru_ba': nrm(15, (DEPTH, LRU_BLOCKS, LRU_BLOCK), 0.02),
        'lru_wi': nrm(16, (DEPTH, LRU_BLOCKS, LRU_BLOCK, LRU_BLOCK), LRU_BLOCK ** -0.5),
        'lru_bi': nrm(17, (DEPTH, LRU_BLOCKS, LRU_BLOCK), 0.02),
        'lru_lambda': lam,
        'dil_qn_g': 1.0 + nrm(18, (DEPTH, HEAD_DIM), 0.05),
        'dil_kn_g': 1.0 + nrm(19, (DEPTH, HEAD_DIM), 0.05),
        'mem_norm_g': 1.0 + nrm(20, (DEPTH, D_MODEL), 0.05),
        'w_mem_kv': nrm(21, (DEPTH, D_MODEL, 2 * MEM_WIDTH), D_MODEL ** -0.5),
        'mem_qn_g': 1.0 + nrm(22, (DEPTH, MEM_HEAD_DIM), 0.05),
        'mem_kn_g': 1.0 + nrm(23, (DEPTH, MEM_HEAD_DIM), 0.05),
        'w_br_a': nrm(24, (DEPTH, LRU_WIDTH, D_MODEL), LRU_WIDTH ** -0.5),
        'w_br_b': nrm(25, (DEPTH, DIL_OUT, D_MODEL), DIL_OUT ** -0.5),
        'w_br_m': nrm(26, (DEPTH, MEM_WIDTH, D_MODEL), MEM_WIDTH ** -0.5),
        'w_out': nrm(27, (DEPTH, D_MODEL, D_MODEL), D_MODEL ** -0.5),
        'norm2_g': 1.0 + nrm(28, (DEPTH, D_MODEL), 0.05),
        'peer_wq': nrm(29, (DEPTH, D_MODEL, PEER_HEADS * PEER_DQ), D_MODEL ** -0.5),
        'peer_subkeys': nrm(30, (DEPTH, PEER_HEADS, 2, PEER_NKEYS, PEER_DQ // 2), (PEER_DQ // 2) ** -0.5),
        'peer_u': nrm(31, (DEPTH, PEER_N, D_MODEL), D_MODEL ** -0.5),
        'peer_v': nrm(32, (DEPTH, PEER_N, D_MODEL), 0.1),
    }


def reference(x_prompt, x_sample, state_conv, state_rnn, cache_swa0, cache_swa1, cache_swa2, cache_mem,
              mem_prompt, rel_bias, norm1_g, w_in, conv_w, conv_b, lru_wa, lru_ba, lru_wi, lru_bi,
              lru_lambda, dil_qn_g, dil_kn_g, mem_norm_g, w_mem_kv, mem_qn_g, mem_kn_g, w_br_a, w_br_b,
              w_br_m, w_out, norm2_g, peer_wq, peer_subkeys, peer_u, peer_v):
    xp = x_prompt
    xs = x_sample
    pc, pr, p0, p1, p2, pm = [], [], [], [], [], []
    sc, sr, s0, s1, s2 = [], [], [], [], []
    for l in range(DEPTH):
        lp = {
            'norm1_g': norm1_g[l], 'w_in': w_in[l], 'conv_w': conv_w[l], 'conv_b': conv_b[l],
            'lru_wa': lru_wa[l], 'lru_ba': lru_ba[l], 'lru_wi': lru_wi[l], 'lru_bi': lru_bi[l],
            'lru_lambda': lru_lambda[l], 'dil_qn_g': dil_qn_g[l], 'dil_kn_g': dil_kn_g[l],
            'mem_norm_g': mem_norm_g[l], 'w_mem_kv': w_mem_kv[l], 'mem_qn_g': mem_qn_g[l],
            'mem_kn_g': mem_kn_g[l], 'w_br_a': w_br_a[l], 'w_br_b': w_br_b[l], 'w_br_m': w_br_m[l],
            'w_out': w_out[l], 'norm2_g': norm2_g[l], 'peer_wq': peer_wq[l],
            'peer_subkeys': peer_subkeys[l], 'peer_u': peer_u[l], 'peer_v': peer_v[l],
        }
        xp, c_p, h_p, bufs_p, mkv_p = layer_prompt(xp, mem_prompt, rel_bias, lp)
        xs, c_s, h_s, bufs_s = layer_sample(xs, state_conv[l], state_rnn[l],
                                            (cache_swa0[l], cache_swa1[l], cache_swa2[l]),
                                            cache_mem[l], rel_bias, lp)
        pc.append(c_p)
        pr.append(h_p)
        p0.append(bufs_p[0])
        p1.append(bufs_p[1])
        p2.append(bufs_p[2])
        pm.append(mkv_p)
        sc.append(c_s)
        sr.append(h_s)
        s0.append(bufs_s[0])
        s1.append(bufs_s[1])
        s2.append(bufs_s[2])
    p_conv = jnp.stack(pc)
    p_rnn = jnp.stack(pr)
    p_swa0 = jnp.stack(p0)
    p_swa1 = jnp.stack(p1)
    p_swa2 = jnp.stack(p2)
    p_mem = jnp.stack(pm)
    s_conv = jnp.stack(sc)
    s_rnn = jnp.stack(sr)
    s_swa0 = jnp.stack(s0)
    s_swa1 = jnp.stack(s1)
    s_swa2 = jnp.stack(s2)
    return (xp, xs, p_conv, p_rnn, p_swa0, p_swa1, p_swa2, p_mem, s_conv, s_rnn, s_swa0, s_swa1, s_swa2)
```

```python
import functools
import math

import jax
import jax.numpy as jnp
from jax import lax
from jax.experimental import pallas as pl
from jax.experimental.pallas import tpu as pltpu

F32 = jnp.float32
BF16 = jnp.bfloat16

D_MODEL = 4096
HEAD_DIM = 128
LRU_WIDTH = 1536
LRU_BLOCKS = 12
LRU_BLOCK = LRU_WIDTH // LRU_BLOCKS
CONV_WIDTH = 4
LRU_C = 8.0
DIL_PAIRS = ((128, 1), (512, 4), (2048, 16))
N_GROUPS = len(DIL_PAIRS)
HPG = 4
DIL_HEADS = N_GROUPS * HPG
DIL_WIDTH = DIL_HEADS * HEAD_DIM
DIL_OUT = HPG * HEAD_DIM
BAND = 128
DIL_SCALE = HEAD_DIM ** -0.5
N_BUCKETS = 32
MAX_DISTANCE = 2048
MEM_HEADS = 4
MEM_HEAD_DIM = 256
MEM_WIDTH = MEM_HEADS * MEM_HEAD_DIM
MEM_SCALE = MEM_HEAD_DIM ** -0.5
N_BRANCH = 3
PEER_HEADS = 8
PEER_NKEYS = 128
PEER_N = PEER_NKEYS * PEER_NKEYS
PEER_DQ = 256
PEER_TOPK = 16
EPS = 1e-6

O_LRU_X = 0
O_LRU_G = O_LRU_X + LRU_WIDTH
O_Q = O_LRU_G + LRU_WIDTH
O_K = O_Q + DIL_WIDTH
O_V = O_K + DIL_WIDTH
O_MQ = O_V + DIL_WIDTH
O_GATE = O_MQ + MEM_WIDTH
IN_COLS = O_GATE + N_BRANCH * D_MODEL

V7X_VMEM_LIMIT_BYTES = 58 * 1024 * 1024
LANES = 128
SUBLANES = 8
MASK_VALUE = -1e30
ROW_PAD = 512


def _params(*sem):
    return pltpu.CompilerParams(dimension_semantics=sem, vmem_limit_bytes=V7X_VMEM_LIMIT_BYTES)


def _rms(x, g):
    return x * lax.rsqrt(jnp.mean(x * x, axis=-1, keepdims=True) + EPS) * g


def _bf16_round(x):
    return x.astype(BF16).astype(F32)


def _rms_cast_kernel(x_ref, g_ref, o_ref):
    o_ref[...] = _rms(x_ref[...], g_ref[...]).astype(o_ref.dtype)


def rms_cast(x, g, bm):
    m, d = x.shape
    return pl.pallas_call(
        _rms_cast_kernel,
        grid=(m // bm,),
        in_specs=[pl.BlockSpec((bm, d), lambda i: (i, 0)), pl.BlockSpec((1, d), lambda i: (0, 0))],
        out_specs=pl.BlockSpec((bm, d), lambda i: (i, 0)),
        out_shape=jax.ShapeDtypeStruct((m, d), BF16),
        compiler_params=_params("parallel"),
        name="rms_cast",
    )(x, g.reshape(1, d))


def _mm_kernel(a_ref, w_ref, o_ref):
    o_ref[...] = jnp.dot(a_ref[...], w_ref[...], preferred_element_type=F32).astype(o_ref.dtype)


def _mm_res_kernel(a_ref, w_ref, r_ref, o_ref):
    o_ref[...] = r_ref[...] + jnp.dot(a_ref[...], w_ref[...], preferred_element_type=F32)


def matmul(a, w, bm, bn, residual=None, out_dtype=F32, name="matmul"):
    m, k = a.shape
    n = w.shape[1]
    in_specs = [pl.BlockSpec((bm, k), lambda i, j: (i, 0)), pl.BlockSpec((k, bn), lambda i, j: (0, j))]
    args = [a, w]
    body = _mm_kernel
    if residual is not None:
        in_specs.append(pl.BlockSpec((bm, bn), lambda i, j: (i, j)))
        args.append(residual)
        body = _mm_res_kernel
    return pl.pallas_call(
        body,
        grid=(m // bm, n // bn),
        in_specs=in_specs,
        out_specs=pl.BlockSpec((bm, bn), lambda i, j: (i, j)),
        out_shape=jax.ShapeDtypeStruct((m, n), out_dtype),
        compiler_params=_params("parallel", "arbitrary"),
        name=name,
    )(*args)


def _lru_gates(xc, wai_ref, bai_ref, sp_ref):
    a_parts, b_parts = [], []
    for n in range(LRU_BLOCKS):
        cs = slice(n * LRU_BLOCK, (n + 1) * LRU_BLOCK)
        xn = xc[:, cs]
        ri = jnp.dot(xn.astype(BF16), wai_ref[n], preferred_element_type=F32) + bai_ref[n]
        r = jax.nn.sigmoid(ri[:, :LRU_BLOCK])
        i = jax.nn.sigmoid(ri[:, LRU_BLOCK:])
        log_a = -LRU_C * r * sp_ref[:, cs]
        a_parts.append(jnp.exp(log_a))
        th = jnp.tanh(log_a)
        b_parts.append(jnp.sqrt(-2.0 * th / (1.0 - th)) * (i * xn))
    return a_parts, b_parts


def _lru_prompt_kernel(x_ref, gate_ref, cw_ref, cb_ref, wai_ref, bai_ref, sp_ref,
                       y_ref, tail_ref, hl_ref, xbuf, sa, sb, hcar):
    tt = pl.program_id(1)
    t_blk = x_ref.shape[0]

    @pl.when(tt == 0)
    def _():
        xbuf[0:SUBLANES, :] = jnp.zeros((SUBLANES, LRU_WIDTH), F32)
        hcar[...] = jnp.zeros((1, LRU_WIDTH), F32)

    x = x_ref[...]
    xbuf[SUBLANES:SUBLANES + t_blk, :] = x
    base = SUBLANES - (CONV_WIDTH - 1)
    xc = cb_ref[...] + xbuf[base:base + t_blk, :] * cw_ref[0:1, :]
    for k in range(1, CONV_WIDTH):
        xc = xc + xbuf[base + k:base + k + t_blk, :] * cw_ref[k:k + 1, :]
    a_parts, b_parts = _lru_gates(xc, wai_ref, bai_ref, sp_ref)
    for n in range(LRU_BLOCKS):
        cs = slice(n * LRU_BLOCK, (n + 1) * LRU_BLOCK)
        sa[:, cs] = a_parts[n]
        sb[:, cs] = b_parts[n]

    row = lax.broadcasted_iota(jnp.int32, (SUBLANES, LRU_WIDTH), 0)

    def group(gi, h_prev):
        r0 = pl.multiple_of(gi * SUBLANES, SUBLANES)
        a = sa[pl.ds(r0, SUBLANES), :]
        b = sb[pl.ds(r0, SUBLANES), :]
        for s in (1, 2, 4):
            a_up = pltpu.roll(a, s, axis=0)
            b_up = pltpu.roll(b, s, axis=0)
            keep = row >= s
            b = jnp.where(keep, a * b_up + b, b)
            a = jnp.where(keep, a * a_up, a)
        h = a * h_prev + b
        sb[pl.ds(r0, SUBLANES), :] = h
        return h[SUBLANES - 1:SUBLANES, :]

    h_last = lax.fori_loop(0, t_blk // SUBLANES, group, hcar[...])
    hcar[...] = h_last
    h = sb[...]
    y_ref[...] = (h * jax.nn.gelu(gate_ref[...])).astype(y_ref.dtype)
    xbuf[0:SUBLANES, :] = x[t_blk - SUBLANES:, :]
    tail_ref[0] = x[t_blk - SUBLANES:, :]
    hl_ref[0] = h[t_blk - SUBLANES:, :]


def lru_prompt(z, batch, seq, cw, cb, wai, bai, sp, t_blk):
    nt = seq // t_blk
    w = LRU_WIDTH
    full = lambda shape: pl.BlockSpec(shape, lambda b, t: (0,) * len(shape))
    return pl.pallas_call(
        _lru_prompt_kernel,
        grid=(batch, nt),
        in_specs=[
            pl.BlockSpec((t_blk, w), lambda b, t: (b * nt + t, O_LRU_X // w)),
            pl.BlockSpec((t_blk, w), lambda b, t: (b * nt + t, O_LRU_G // w)),
            full((CONV_WIDTH, w)), full((1, w)), full((LRU_BLOCKS, LRU_BLOCK, 2 * LRU_BLOCK)),
            full((LRU_BLOCKS, 1, 2 * LRU_BLOCK)), full((1, w)),
        ],
        out_specs=[
            pl.BlockSpec((t_blk, w), lambda b, t: (b * nt + t, 0)),
            pl.BlockSpec((1, SUBLANES, w), lambda b, t: (b, 0, 0)),
            pl.BlockSpec((1, SUBLANES, w), lambda b, t: (b, 0, 0)),
        ],
        out_shape=[
            jax.ShapeDtypeStruct((batch * seq, w), BF16),
            jax.ShapeDtypeStruct((batch, SUBLANES, w), F32),
            jax.ShapeDtypeStruct((batch, SUBLANES, w), F32),
        ],
        scratch_shapes=[
            pltpu.VMEM((t_blk + SUBLANES, w), F32), pltpu.VMEM((t_blk, w), F32),
            pltpu.VMEM((t_blk, w), F32), pltpu.VMEM((1, w), F32),
        ],
        compiler_params=_params("parallel", "arbitrary"),
        name="lru_prompt",
    )(z, z, cw, cb, wai, bai, sp)


def _lru_sample_kernel(x_ref, gate_ref, c0_ref, c1_ref, c2_ref, h0_ref, cw_ref, cb_ref, wai_ref, bai_ref, sp_ref,
                       y_ref, h_ref):
    x = x_ref[...]
    xc = cb_ref[...] + c0_ref[...] * cw_ref[0:1, :]
    xc = xc + c1_ref[...] * cw_ref[1:2, :]
    xc = xc + c2_ref[...] * cw_ref[2:3, :]
    xc = xc + x * cw_ref[3:4, :]
    a_parts, b_parts = _lru_gates(xc, wai_ref, bai_ref, sp_ref)
    for n in range(LRU_BLOCKS):
        cs = slice(n * LRU_BLOCK, (n + 1) * LRU_BLOCK)
        h = a_parts[n] * h0_ref[:, cs] + b_parts[n]
        h_ref[:, cs] = h
        y_ref[:, cs] = (h * jax.nn.gelu(gate_ref[:, cs])).astype(y_ref.dtype)


def lru_sample(x, gate, c0, c1, c2, h0, cw, cb, wai, bai, sp):
    nb, w = x.shape
    return pl.pallas_call(
        _lru_sample_kernel,
        out_shape=[jax.ShapeDtypeStruct((nb, w), BF16), jax.ShapeDtypeStruct((nb, w), F32)],
        compiler_params=pltpu.CompilerParams(vmem_limit_bytes=V7X_VMEM_LIMIT_BYTES),
        name="lru_sample",
    )(x, gate, c0, c1, c2, h0, cw, cb, wai, bai, sp)


def _dil_prompt_kernel(q_ref, k_ref, v_ref, qg_ref, kg_ref, bias_ref, o_ref, lse_ref, kn_ref, qb, kb, vb):
    sub_len = q_ref.shape[0]
    kb[0:BAND, :] = jnp.zeros((BAND, DIL_OUT), BF16)
    vb[0:BAND, :] = jnp.zeros((BAND, DIL_OUT), BF16)
    for h in range(HPG):
        cs = slice(h * HEAD_DIM, (h + 1) * HEAD_DIM)
        qb[:, cs] = _rms(q_ref[:, cs], qg_ref[...]).astype(BF16)
        kn = _rms(k_ref[:, cs], kg_ref[...])
        kn_ref[:, cs] = kn
        kb[BAND:BAND + sub_len, cs] = kn.astype(BF16)
    vb[BAND:BAND + sub_len, :] = v_ref[...].astype(BF16)
    col = lax.broadcasted_iota(jnp.int32, (BAND, 2 * BAND), 1)

    def block(n, carry):
        r0 = pl.multiple_of(n * BAND, BAND)
        no_prev = jnp.logical_and(n == 0, col < BAND)
        for h in range(HPG):
            cs = slice(h * HEAD_DIM, (h + 1) * HEAD_DIM)
            q = qb[pl.ds(r0, BAND), cs]
            kk = kb[pl.ds(r0, 2 * BAND), cs]
            vv = vb[pl.ds(r0, 2 * BAND), cs]
            s = lax.dot_general(q, kk, (((1,), (1,)), ((), ())), preferred_element_type=F32)
            s = s * DIL_SCALE + bias_ref[h]
            s = jnp.where(no_prev, MASK_VALUE, s)
            m = jnp.max(s, axis=-1, keepdims=True)
            p = jnp.exp(s - m)
            den = jnp.sum(p, axis=-1, keepdims=True)
            o = jnp.dot(p.astype(BF16), vv, preferred_element_type=F32) / den
            o_ref[pl.ds(r0, BAND), cs] = o
            lse_ref[pl.ds(r0, BAND), cs] = jnp.broadcast_to(m + jnp.log(den), (BAND, HEAD_DIM))
        return carry

    lax.fori_loop(0, sub_len // BAND, block, 0)


def dil_prompt(z, batch, seq, g, qg, kg, bias):
    dil = DIL_PAIRS[g][1]
    sub_len = seq // dil
    rows = z.shape[0] // dil
    zv = z.reshape(rows, dil * IN_COLS)
    per_res = IN_COLS // DIL_OUT
    spec = lambda off: pl.BlockSpec((sub_len, DIL_OUT), lambda b, r: (b, r * per_res + off // DIL_OUT + g))
    full = lambda shape: pl.BlockSpec(shape, lambda b, r: (0,) * len(shape))
    out_spec = pl.BlockSpec((sub_len, DIL_OUT), lambda b, r: (b, r))
    out_sds = jax.ShapeDtypeStruct((batch * sub_len, dil * DIL_OUT), F32)
    o, lse, kn = pl.pallas_call(
        _dil_prompt_kernel,
        grid=(batch, dil),
        in_specs=[spec(O_Q), spec(O_K), spec(O_V), full((1, HEAD_DIM)), full((1, HEAD_DIM)),
                  full((HPG, BAND, 2 * BAND))],
        out_specs=[out_spec, out_spec, out_spec],
        out_shape=[out_sds, out_sds, out_sds],
        scratch_shapes=[pltpu.VMEM((sub_len, DIL_OUT), BF16), pltpu.VMEM((sub_len + BAND, DIL_OUT), BF16),
                        pltpu.VMEM((sub_len + BAND, DIL_OUT), BF16)],
        compiler_params=_params("parallel", "arbitrary"),
        name=f"dil_prompt_g{g}",
    )(zv, zv, zv, qg, kg, bias)
    n = batch * seq
    return o.reshape(n, DIL_OUT), lse.reshape(n, DIL_OUT), kn.reshape(n, DIL_OUT)


def _combine_kernel(o0, o1, o2, l0, l1, l2, y_ref):
    a0, a1, a2 = l0[...], l1[...], l2[...]
    m = jnp.maximum(jnp.maximum(a0, a1), a2)
    e0, e1, e2 = jnp.exp(a0 - m), jnp.exp(a1 - m), jnp.exp(a2 - m)
    tot = e0 + e1 + e2
    y = (e0 / tot) * o0[...] + (e1 / tot) * o1[...] + (e2 / tot) * o2[...]
    y_ref[...] = y.astype(y_ref.dtype)


def combine_groups(outs, lses, bm):
    n, w = outs[0].shape
    spec = pl.BlockSpec((bm, w), lambda i: (i, 0))
    return pl.pallas_call(
        _combine_kernel,
        grid=(n // bm,),
        in_specs=[spec] * 6,
        out_specs=spec,
        out_shape=jax.ShapeDtypeStruct((n, w), BF16),
        compiler_params=_params("parallel"),
        name="combine_groups",
    )(*outs, *lses)


def _dil_sample_kernel(q_ref, k_ref, v_ref, c0_ref, c1_ref, c2_ref, qg_ref, kg_ref, bcol_ref, b0_ref,
                       y_ref, kn_ref):
    caches = (c0_ref, c1_ref, c2_ref)
    outs = [[None] * N_GROUPS for _ in range(HPG)]
    lses = [[None] * N_GROUPS for _ in range(HPG)]
    for g in range(N_GROUPS):
        c_ref = caches[g]
        for hh in range(HPG):
            h = g * HPG + hh
            cs = slice(h * HEAD_DIM, (h + 1) * HEAD_DIM)
            q = _bf16_round(_rms(q_ref[:, :, cs], qg_ref[...]))
            kn = _rms(k_ref[:, :, cs], kg_ref[...])
            kn_ref[:, :, cs] = kn
            k_new = _bf16_round(kn)
            v_new = _bf16_round(v_ref[:, :, cs])
            kc = _bf16_round(c_ref[:, :, hh * HEAD_DIM:(hh + 1) * HEAD_DIM])
            vc = _bf16_round(c_ref[:, :, DIL_OUT + hh * HEAD_DIM:DIL_OUT + (hh + 1) * HEAD_DIM])
            s = jnp.sum(kc * q, axis=-1, keepdims=True) * DIL_SCALE + bcol_ref[h]
            s0 = jnp.sum(k_new * q, axis=-1, keepdims=True) * DIL_SCALE + b0_ref[h]
            m = jnp.maximum(jnp.max(s, axis=1, keepdims=True), s0)
            p = jnp.exp(s - m)
            p0 = jnp.exp(s0 - m)
            den = jnp.sum(p, axis=1, keepdims=True) + p0
            acc = jnp.sum(_bf16_round(p) * vc, axis=1, keepdims=True) + _bf16_round(p0) * v_new
            outs[hh][g] = acc / den
            lses[hh][g] = m + jnp.log(den)
    for hh in range(HPG):
        l0, l1, l2 = lses[hh]
        m = jnp.maximum(jnp.maximum(l0, l1), l2)
        e0, e1, e2 = jnp.exp(l0 - m), jnp.exp(l1 - m), jnp.exp(l2 - m)
        tot = e0 + e1 + e2
        y = (e0 / tot) * outs[hh][0] + (e1 / tot) * outs[hh][1] + (e2 / tot) * outs[hh][2]
        y_ref[:, :, hh * HEAD_DIM:(hh + 1) * HEAD_DIM] = y.astype(y_ref.dtype)


def dil_sample(zs3, caches, qg, kg, bcol, b0, bb):
    nb = zs3.shape[0]
    zspec = lambda off: pl.BlockSpec((bb, 1, DIL_WIDTH), lambda i: (i, 0, off // DIL_WIDTH))
    full = lambda shape: pl.BlockSpec(shape, lambda i: (0,) * len(shape))
    cviews, cspecs = [], []
    for g in range(N_GROUPS):
        dil = DIL_PAIRS[g][1]
        cviews.append(caches[g].reshape(nb, BAND, dil * 2 * DIL_OUT))
        cspecs.append(pl.BlockSpec((bb, BAND, 2 * DIL_OUT), lambda i: (i, 0, 0)))
    return pl.pallas_call(
        _dil_sample_kernel,
        grid=(nb // bb,),
        in_specs=[zspec(O_Q), zspec(O_K), zspec(O_V), *cspecs, full((1, 1, HEAD_DIM)), full((1, 1, HEAD_DIM)),
                  full((DIL_HEADS, BAND, 1)), full((DIL_HEADS, 1, 1))],
        out_specs=[pl.BlockSpec((bb, 1, DIL_OUT), lambda i: (i, 0, 0)),
                   pl.BlockSpec((bb, 1, DIL_WIDTH), lambda i: (i, 0, 0))],
        out_shape=[jax.ShapeDtypeStruct((nb, 1, DIL_OUT), BF16), jax.ShapeDtypeStruct((nb, 1, DIL_WIDTH), F32)],
        compiler_params=_params("parallel"),
        name="dil_sample",
    )(zs3, zs3, zs3, *cviews, qg, kg, bcol, b0)


def _mem_knorm_kernel(kv_ref, g_ref, o_ref):
    for h in range(MEM_HEADS):
        cs = slice(h * MEM_HEAD_DIM, (h + 1) * MEM_HEAD_DIM)
        o_ref[:, cs] = _rms(kv_ref[:, cs], g_ref[...])
    o_ref[:, MEM_WIDTH:] = kv_ref[:, MEM_WIDTH:]


def mem_knorm(kv, g, bm):
    m, w = kv.shape
    return pl.pallas_call(
        _mem_knorm_kernel,
        grid=(m // bm,),
        in_specs=[pl.BlockSpec((bm, w), lambda i: (i, 0)), pl.BlockSpec((1, MEM_HEAD_DIM), lambda i: (0, 0))],
        out_specs=pl.BlockSpec((bm, w), lambda i: (i, 0)),
        out_shape=jax.ShapeDtypeStruct((m, w), F32),
        compiler_params=_params("parallel"),
        name="mem_knorm",
    )(kv, g)


def _mem_prompt_kernel(qa_ref, qb_ref, kv_ref, g_ref, y_ref):
    for h in range(MEM_HEADS):
        q_ref = qa_ref if h < 2 else qb_ref
        qs = slice((h % 2) * MEM_HEAD_DIM, (h % 2 + 1) * MEM_HEAD_DIM)
        cs = slice(h * MEM_HEAD_DIM, (h + 1) * MEM_HEAD_DIM)
        q = _rms(q_ref[:, qs], g_ref[...]).astype(BF16)
        k = kv_ref[:, cs].astype(BF16)
        v = kv_ref[:, MEM_WIDTH + h * MEM_HEAD_DIM:MEM_WIDTH + (h + 1) * MEM_HEAD_DIM].astype(BF16)
        s = lax.dot_general(q, k, (((1,), (1,)), ((), ())), preferred_element_type=F32) * MEM_SCALE
        m = jnp.max(s, axis=-1, keepdims=True)
        e = jnp.exp(s - m)
        p = e / jnp.sum(e, axis=-1, keepdims=True)
        y_ref[:, cs] = jnp.dot(p.astype(BF16), v, preferred_element_type=F32).astype(y_ref.dtype)


def mem_attend_prompt(z, mkv, batch, seq, g, tq):
    nq = seq // tq
    mt = mkv.shape[0] // batch
    half = 2 * MEM_HEAD_DIM
    return pl.pallas_call(
        _mem_prompt_kernel,
        grid=(batch, nq),
        in_specs=[pl.BlockSpec((tq, half), lambda b, t: (b * nq + t, O_MQ // half)),
                  pl.BlockSpec((tq, half), lambda b, t: (b * nq + t, O_MQ // half + 1)),
                  pl.BlockSpec((mt, 2 * MEM_WIDTH), lambda b, t: (b, 0)),
                  pl.BlockSpec((1, MEM_HEAD_DIM), lambda b, t: (0, 0))],
        out_specs=pl.BlockSpec((tq, MEM_WIDTH), lambda b, t: (b * nq + t, 0)),
        out_shape=jax.ShapeDtypeStruct((batch * seq, MEM_WIDTH), BF16),
        compiler_params=_params("parallel", "arbitrary"),
        name="mem_prompt",
    )(z, z, mkv, g)


def _mem_sample_kernel(qa_ref, qb_ref, kv_ref, g_ref, y_ref):
    for h in range(MEM_HEADS):
        q_ref = qa_ref if h < 2 else qb_ref
        qs = slice((h % 2) * MEM_HEAD_DIM, (h % 2 + 1) * MEM_HEAD_DIM)
        cs = slice(h * MEM_HEAD_DIM, (h + 1) * MEM_HEAD_DIM)
        q = _bf16_round(_rms(q_ref[:, :, qs], g_ref[...]))
        k = _bf16_round(kv_ref[:, :, cs])
        v = _bf16_round(kv_ref[:, :, MEM_WIDTH + h * MEM_HEAD_DIM:MEM_WIDTH + (h + 1) * MEM_HEAD_DIM])
        s = jnp.sum(k * q, axis=-1, keepdims=True) * MEM_SCALE
        m = jnp.max(s, axis=1, keepdims=True)
        e = jnp.exp(s - m)
        p = e / jnp.sum(e, axis=1, keepdims=True)
        y_ref[:, :, cs] = jnp.sum(_bf16_round(p) * v, axis=1, keepdims=True).astype(y_ref.dtype)


def mem_sample(zs3, cache_mem, g, bb):
    nb = zs3.shape[0]
    mt = cache_mem.shape[1]
    half = 2 * MEM_HEAD_DIM
    return pl.pallas_call(
        _mem_sample_kernel,
        grid=(nb // bb,),
        in_specs=[pl.BlockSpec((bb, 1, half), lambda i: (i, 0, O_MQ // half)),
                  pl.BlockSpec((bb, 1, half), lambda i: (i, 0, O_MQ // half + 1)),
                  pl.BlockSpec((bb, mt, 2 * MEM_WIDTH), lambda i: (i, 0, 0)),
                  pl.BlockSpec((1, 1, MEM_HEAD_DIM), lambda i: (0, 0, 0))],
        out_specs=pl.BlockSpec((bb, 1, MEM_WIDTH), lambda i: (i, 0, 0)),
        out_shape=jax.ShapeDtypeStruct((nb, 1, MEM_WIDTH), BF16),
        compiler_params=_params("parallel"),
        name="mem_sample",
    )(zs3, zs3, cache_mem.reshape(nb, mt, 2 * MEM_WIDTH), g)


def _merge_kernel(ya_ref, yb_ref, ym_ref, wa_ref, wb_ref, wm_ref, g0_ref, g1_ref, g2_ref, o_ref):
    ma = jnp.dot(ya_ref[...], wa_ref[...], preferred_element_type=F32)
    mb = jnp.dot(yb_ref[...], wb_ref[...], preferred_element_type=F32)
    mm = jnp.dot(ym_ref[...], wm_ref[...], preferred_element_type=F32)
    merged = jax.nn.sigmoid(g0_ref[...]) * ma + jax.nn.sigmoid(g1_ref[...]) * mb + jax.nn.sigmoid(g2_ref[...]) * mm
    o_ref[...] = merged.astype(o_ref.dtype)


def merge(ya, yb, ym, wa, wb, wm, z, bm, bn):
    m = ya.shape[0]
    nj = D_MODEL // bn
    gate = lambda g: pl.BlockSpec((bm, bn), lambda i, j: (i, O_GATE // bn + g * nj + j))
    row = lambda k: pl.BlockSpec((bm, k), lambda i, j: (i, 0))
    col = lambda k: pl.BlockSpec((k, bn), lambda i, j: (0, j))
    return pl.pallas_call(
        _merge_kernel,
        grid=(m // bm, nj),
        in_specs=[row(LRU_WIDTH), row(DIL_OUT), row(MEM_WIDTH), col(LRU_WIDTH), col(DIL_OUT), col(MEM_WIDTH),
                  gate(0), gate(1), gate(2)],
        out_specs=pl.BlockSpec((bm, bn), lambda i, j: (i, j)),
        out_shape=jax.ShapeDtypeStruct((m, D_MODEL), BF16),
        compiler_params=_params("parallel", "arbitrary"),
        name="merge",
    )(ya, yb, ym, wa, wb, wm, z, z, z)


PEER_RANKS = PEER_TOPK + 1


def _candidate_pairs():
    return [(a, b) for a in range(PEER_RANKS) for b in range(PEER_RANKS) if (a + 1) * (b + 1) <= PEER_RANKS]


PEER_PAIRS = _candidate_pairs()
PEER_CAND_ROWS = -(-len(PEER_PAIRS) // SUBLANES) * SUBLANES


def _extract_top(c, count):
    rows = c.shape[0]
    idx = lax.broadcasted_iota(jnp.int32, c.shape, 0)
    out = []
    for _ in range(count):
        m = jnp.max(c, axis=0, keepdims=True)
        first = jnp.min(jnp.where(c == m, idx, rows), axis=0, keepdims=True)
        c = jnp.where(idx == first, -jnp.inf, c)
        out.append(m)
    return out


def _peer_stats_kernel(q_ref, sk_ref, s2_ref, e2_ref, thr_ref, e1_ref, cand):
    tm = q_ref.shape[0]
    half = PEER_DQ // 2
    for h in range(PEER_HEADS):
        st = []
        for p in range(2):
            c0 = (h * 2 + p) * half
            qhp = q_ref[:, c0:c0 + half].astype(BF16)
            st.append(lax.dot_general(sk_ref[h * 2 + p], qhp, (((1,), (1,)), ((), ())),
                                      preferred_element_type=F32))
        s1, s2 = st
        v1 = _extract_top(s1, PEER_RANKS)
        v2 = _extract_top(s2, PEER_RANKS)
        cand[...] = jnp.full((PEER_CAND_ROWS, tm), -jnp.inf, F32)
        for r, (a, b) in enumerate(PEER_PAIRS):
            cand[r:r + 1, :] = v1[a] + v2[b]
        top = _extract_top(cand[...], PEER_RANKS)
        zsum = jnp.ones_like(top[0])
        for r in range(1, PEER_TOPK):
            zsum = zsum + jnp.exp(top[r] - top[0])
        tau = 0.5 * (top[PEER_TOPK - 1] + top[PEER_TOPK])
        s2_ref[h] = s2
        e2_ref[h] = jnp.exp(s2 - v2[0]) / zsum
        thr_ref[h] = tau - s1
        e1_ref[h] = jnp.exp(s1 - v1[0])


def peer_stats(q, sk, tm):
    m = q.shape[0]
    out_sds = jax.ShapeDtypeStruct((PEER_HEADS, PEER_NKEYS, m), F32)
    out_spec = pl.BlockSpec((PEER_HEADS, PEER_NKEYS, tm), lambda i: (0, 0, i))
    return pl.pallas_call(
        _peer_stats_kernel,
        grid=(m // tm,),
        in_specs=[pl.BlockSpec((tm, PEER_HEADS * PEER_DQ), lambda i: (i, 0)),
                  pl.BlockSpec((2 * PEER_HEADS, PEER_NKEYS, PEER_DQ // 2), lambda i: (0, 0, 0))],
        out_specs=[out_spec] * 4,
        out_shape=[out_sds] * 4,
        scratch_shapes=[pltpu.VMEM((PEER_CAND_ROWS, tm), F32)],
        compiler_params=_params("parallel"),
        name="peer_stats",
    )(q, sk)


def _peer_dense_kernel(h_ref, u_ref, v_ref, s2_ref, e2_ref, thr_ref, e1_ref, o_ref, *, tn):
    e = pl.program_id(1)
    tm = h_ref.shape[0]

    @pl.when(e == 0)
    def _():
        o_ref[...] = jnp.zeros(o_ref.shape, F32)

    nj = tn // PEER_NKEYS
    gates = []
    for j in range(nj):
        i1 = e * nj + j
        gt = jnp.zeros((PEER_NKEYS, tm), F32)
        for h in range(PEER_HEADS):
            thr = thr_ref[h, pl.ds(i1, 1), :]
            e1 = e1_ref[h, pl.ds(i1, 1), :]
            gt = gt + e1 * jnp.where(s2_ref[h] >= thr, e2_ref[h], 0.0)
        gates.append(gt.T)
    gate = gates[0] if nj == 1 else jnp.concatenate(gates, axis=1)
    act = lax.dot_general(h_ref[...], u_ref[...], (((1,), (1,)), ((), ())), preferred_element_type=F32)
    w = (gate * jax.nn.gelu(act)).astype(BF16)
    o_ref[...] += jnp.dot(w, v_ref[...], preferred_element_type=F32)


def peer_dense(h2, u, v, stats, tm, tn):
    m = h2.shape[0]
    stat_spec = pl.BlockSpec((PEER_HEADS, PEER_NKEYS, tm), lambda i, e: (0, 0, i))
    return pl.pallas_call(
        functools.partial(_peer_dense_kernel, tn=tn),
        grid=(m // tm, PEER_N // tn),
        in_specs=[pl.BlockSpec((tm, D_MODEL), lambda i, e: (i, 0)),
                  pl.BlockSpec((tn, D_MODEL), lambda i, e: (e, 0)),
                  pl.BlockSpec((tn, D_MODEL), lambda i, e: (e, 0)),
                  stat_spec, stat_spec, stat_spec, stat_spec],
        out_specs=pl.BlockSpec((tm, D_MODEL), lambda i, e: (i, 0)),
        out_shape=jax.ShapeDtypeStruct((m, D_MODEL), F32),
        compiler_params=_params("parallel", "arbitrary"),
        name="peer_dense",
    )(h2, u, v, *stats)


def _t5_bucket(dist):
    max_exact = N_BUCKETS // 2
    d = jnp.maximum(dist, 1).astype(F32)
    large = max_exact + (jnp.log(d / max_exact) / math.log(MAX_DISTANCE / max_exact)
                         * (N_BUCKETS - max_exact)).astype(jnp.int32)
    large = jnp.minimum(large, N_BUCKETS - 1)
    return jnp.where(dist < max_exact, dist, large)


def _bias_tables(rel_bias):
    qi = jnp.arange(BAND)[:, None]
    ki = jnp.arange(2 * BAND)[None, :]
    off = qi + BAND - ki
    valid = (off >= 0) & (off <= BAND)
    band, cols = [], []
    for g in range(N_GROUPS):
        dil = DIL_PAIRS[g][1]
        dist = jnp.arange(BAND + 1, dtype=jnp.int32) * dil
        bj = rel_bias[_t5_bucket(dist)][:, g * HPG:(g + 1) * HPG].astype(F32)
        tb = bj[jnp.clip(off, 0, BAND)].transpose(2, 0, 1)
        band.append(jnp.where(valid[None], tb, MASK_VALUE))
        cols.append(bj.T)
    return band, jnp.concatenate(cols, axis=0)


def kernel(x_prompt, x_sample, state_conv, state_rnn, cache_swa0, cache_swa1, cache_swa2, cache_mem, mem_prompt, rel_bias, norm1_g, w_in, conv_w, conv_b, lru_wa, lru_ba, lru_wi, lru_bi, lru_lambda, dil_qn_g, dil_kn_g, mem_norm_g, w_mem_kv, mem_qn_g, mem_kn_g, w_br_a, w_br_b, w_br_m, w_out, norm2_g, peer_wq, peer_subkeys, peer_u, peer_v):
    batch, seq, d = x_prompt.shape
    nb = x_sample.shape[0]
    n_p = batch * seq
    m_all = -(-(n_p + nb) // ROW_PAD) * ROW_PAD
    bm_big = m_all // 8 if (m_all // 8) % 16 == 0 else ROW_PAD
    caches = (cache_swa0[0], cache_swa1[0], cache_swa2[0])

    w_in_b = w_in[0].astype(BF16)
    wai = jnp.concatenate([lru_wa[0], lru_wi[0]], axis=-1).astype(BF16)
    bai = jnp.concatenate([lru_ba[0], lru_bi[0]], axis=-1).reshape(LRU_BLOCKS, 1, 2 * LRU_BLOCK)
    sp = jax.nn.softplus(-lru_lambda[0].astype(F32)).reshape(1, LRU_WIDTH)
    cw, cb = conv_w[0], conv_b[0].reshape(1, LRU_WIDTH)
    qg, kg = dil_qn_g[0].reshape(1, HEAD_DIM), dil_kn_g[0].reshape(1, HEAD_DIM)
    band_bias, bias_cols = _bias_tables(rel_bias)
    bcol = bias_cols[:, BAND:0:-1].reshape(DIL_HEADS, BAND, 1)
    b0 = bias_cols[:, 0].reshape(DIL_HEADS, 1, 1)

    x_all = jnp.concatenate([x_prompt.reshape(n_p, d), x_sample.reshape(nb, d),
                             jnp.zeros((m_all - n_p - nb, d), F32)], axis=0)
    n1 = rms_cast(x_all, norm1_g[0], ROW_PAD)
    z = matmul(n1, w_in_b, bm_big, 512, name="in_proj")
    zs = z[n_p:n_p + nb]
    zs3 = zs.reshape(nb, 1, IN_COLS)

    ya_p, tail, hl = lru_prompt(z, batch, seq, cw, cb, wai, bai, sp, 512)
    sc0 = state_conv[0]
    ya_s, h_s = lru_sample(zs[:, O_LRU_X:O_LRU_G], zs[:, O_LRU_G:O_Q], sc0[:, 0], sc0[:, 1], sc0[:, 2],
                           state_rnn[0], cw, cb, wai, bai, sp)

    outs, lses, kns = [], [], []
    for g in range(N_GROUPS):
        o, lse, kn = dil_prompt(z, batch, seq, g, qg, kg, band_bias[g])
        outs.append(o)
        lses.append(lse)
        kns.append(kn)
    yb_p = combine_groups(outs, lses, 1024)
    yb_s, kn_s = dil_sample(zs3, caches, qg.reshape(1, 1, HEAD_DIM), kg.reshape(1, 1, HEAD_DIM), bcol, b0, 8)

    mt = mem_prompt.shape[1]
    nm = rms_cast(mem_prompt.reshape(batch * mt, d), mem_norm_g[0], 256)
    kv = matmul(nm, w_mem_kv[0].astype(BF16), 512, 512, name="mem_kv")
    mkv = mem_knorm(kv, mem_kn_g[0].reshape(1, MEM_HEAD_DIM), 256)
    ym_p = mem_attend_prompt(z, mkv, batch, seq, mem_qn_g[0].reshape(1, MEM_HEAD_DIM), 512)
    ym_s = mem_sample(zs3, cache_mem[0], mem_qn_g[0].reshape(1, 1, MEM_HEAD_DIM), 4)

    pad = m_all - n_p - nb
    cat = lambda p, s: jnp.concatenate([p, s.reshape(nb, -1), jnp.zeros((pad, p.shape[1]), p.dtype)], axis=0)
    merged = merge(cat(ya_p, ya_s), cat(yb_p, yb_s), cat(ym_p, ym_s), w_br_a[0].astype(BF16),
                   w_br_b[0].astype(BF16), w_br_m[0].astype(BF16), z, bm_big, 512)
    x1 = matmul(merged, w_out[0].astype(BF16), bm_big, 512, residual=x_all, name="out_proj")

    h2 = rms_cast(x1, norm2_g[0], ROW_PAD)
    q = matmul(h2, peer_wq[0].astype(BF16), bm_big, 512, name="peer_q")
    sk = peer_subkeys[0].reshape(2 * PEER_HEADS, PEER_NKEYS, PEER_DQ // 2).astype(BF16)
    stats = peer_stats(q, sk, ROW_PAD)
    pe = peer_dense(h2, peer_u[0].astype(BF16), peer_v[0].astype(BF16), stats, ROW_PAD, 256)

    y_prompt = (x1[:n_p] + pe[:n_p]).reshape(batch, seq, d)
    y_sample = (x1[n_p:n_p + nb] + pe[n_p:n_p + nb]).reshape(nb, 1, d)

    p_conv = tail[:, SUBLANES - (CONV_WIDTH - 1):][None]
    p_rnn = hl[:, SUBLANES - 1][None]
    p_swa, s_swa = [], []
    for g in range(N_GROUPS):
        w = min(DIL_PAIRS[g][0], seq)
        kn = kns[g].reshape(batch, seq, HPG, HEAD_DIM)[:, seq - w:]
        vv = z[:n_p, O_V + g * DIL_OUT:O_V + (g + 1) * DIL_OUT].reshape(batch, seq, HPG, HEAD_DIM)[:, seq - w:]
        p_swa.append(jnp.stack([kn, vv], axis=2)[None])
        k_new = kn_s[:, 0, g * DIL_OUT:(g + 1) * DIL_OUT].reshape(nb, 1, 1, HPG, HEAD_DIM)
        v_new = zs[:, O_V + g * DIL_OUT:O_V + (g + 1) * DIL_OUT].reshape(nb, 1, 1, HPG, HEAD_DIM)
        s_swa.append(jnp.concatenate([caches[g][:, 1:], jnp.concatenate([k_new, v_new], axis=2)], axis=1)[None])
    p_mem = mkv.reshape(batch, mt, 2, MEM_HEADS, MEM_HEAD_DIM)[None]
    s_conv = jnp.concatenate([sc0[:, 1:], zs[:, None, O_LRU_X:O_LRU_G]], axis=1)[None]
    s_rnn = h_s[None]
    return (y_prompt, y_sample, p_conv, p_rnn, p_swa[0], p_swa[1], p_swa[2], p_mem,
            s_conv, s_rnn, s_swa[0], s_swa[1], s_swa[2])
```

```python
import functools
import math

import jax
import jax.numpy as jnp
from jax import lax
from jax.experimental import pallas as pl
from jax.experimental.pallas import tpu as pltpu

F32 = jnp.float32
BF16 = jnp.bfloat16

D_MODEL = 4096
HEAD_DIM = 128
LRU_WIDTH = 1536
LRU_BLOCKS = 12
LRU_BLOCK = LRU_WIDTH // LRU_BLOCKS
CONV_WIDTH = 4
LRU_C = 8.0
DIL_PAIRS = ((128, 1), (512, 4), (2048, 16))
N_GROUPS = len(DIL_PAIRS)
HPG = 4
DIL_HEADS = N_GROUPS * HPG
DIL_WIDTH = DIL_HEADS * HEAD_DIM
DIL_OUT = HPG * HEAD_DIM
BAND = 128
DIL_SCALE = HEAD_DIM ** -0.5
N_BUCKETS = 32
MAX_DISTANCE = 2048
MEM_HEADS = 4
MEM_HEAD_DIM = 256
MEM_WIDTH = MEM_HEADS * MEM_HEAD_DIM
MEM_SCALE = MEM_HEAD_DIM ** -0.5
N_BRANCH = 3
PEER_HEADS = 8
PEER_NKEYS = 128
PEER_N = PEER_NKEYS * PEER_NKEYS
PEER_DQ = 256
PEER_TOPK = 16
EPS = 1e-6

O_LRU_X = 0
O_LRU_G = O_LRU_X + LRU_WIDTH
O_Q = O_LRU_G + LRU_WIDTH
O_K = O_Q + DIL_WIDTH
O_V = O_K + DIL_WIDTH
O_MQ = O_V + DIL_WIDTH
O_GATE = O_MQ + MEM_WIDTH
IN_COLS = O_GATE + N_BRANCH * D_MODEL

V7X_VMEM_LIMIT_BYTES = 58 * 1024 * 1024
LANES = 128
SUBLANES = 8
MASK_VALUE = -1e30
ROW_PAD = 512


def _params(*sem):
    return pltpu.CompilerParams(dimension_semantics=sem, vmem_limit_bytes=V7X_VMEM_LIMIT_BYTES)


def _rms(x, g):
    return x * lax.rsqrt(jnp.mean(x * x, axis=-1, keepdims=True) + EPS) * g


def _bf16_round(x):
    return x.astype(BF16).astype(F32)


def _rms_cast_kernel(x_ref, g_ref, o_ref):
    o_ref[...] = _rms(x_ref[...], g_ref[...]).astype(o_ref.dtype)


def rms_cast(x, g, bm):
    m, d = x.shape
    return pl.pallas_call(
        _rms_cast_kernel,
        grid=(m // bm,),
        in_specs=[pl.BlockSpec((bm, d), lambda i: (i, 0)), pl.BlockSpec((1, d), lambda i: (0, 0))],
        out_specs=pl.BlockSpec((bm, d), lambda i: (i, 0)),
        out_shape=jax.ShapeDtypeStruct((m, d), BF16),
        compiler_params=_params("parallel"),
        name="rms_cast",
    )(x, g.reshape(1, d))


def _mm_kernel(a_ref, w_ref, o_ref):
    o_ref[...] = jnp.dot(a_ref[...], w_ref[...], preferred_element_type=F32).astype(o_ref.dtype)


def _mm_res_kernel(a_ref, w_ref, r_ref, o_ref):
    o_ref[...] = r_ref[...] + jnp.dot(a_ref[...], w_ref[...], preferred_element_type=F32)


def matmul(a, w, bm, bn, residual=None, out_dtype=F32, name="matmul"):
    m, k = a.shape
    n = w.shape[1]
    in_specs = [pl.BlockSpec((bm, k), lambda i, j: (i, 0)), pl.BlockSpec((k, bn), lambda i, j: (0, j))]
    args = [a, w]
    body = _mm_kernel
    if residual is not None:
        in_specs.append(pl.BlockSpec((bm, bn), lambda i, j: (i, j)))
        args.append(residual)
        body = _mm_res_kernel
    return pl.pallas_call(
        body,
        grid=(m // bm, n // bn),
        in_specs=in_specs,
        out_specs=pl.BlockSpec((bm, bn), lambda i, j: (i, j)),
        out_shape=jax.ShapeDtypeStruct((m, n), out_dtype),
        compiler_params=_params("parallel", "arbitrary"),
        name=name,
    )(*args)


def _lru_gates(xc, wai_ref, bai_ref, sp_ref):
    a_parts, b_parts = [], []
    for n in range(LRU_BLOCKS):
        cs = slice(n * LRU_BLOCK, (n + 1) * LRU_BLOCK)
        xn = xc[:, cs]
        ri = jnp.dot(xn.astype(BF16), wai_ref[n], preferred_element_type=F32) + bai_ref[n]
        r = jax.nn.sigmoid(ri[:, :LRU_BLOCK])
        i = jax.nn.sigmoid(ri[:, LRU_BLOCK:])
        log_a = -LRU_C * r * sp_ref[:, cs]
        a_parts.append(jnp.exp(log_a))
        th = jnp.tanh(log_a)
        b_parts.append(jnp.sqrt(-2.0 * th / (1.0 - th)) * (i * xn))
    return a_parts, b_parts


def _lru_prompt_kernel(x_ref, gate_ref, cw_ref, cb_ref, wai_ref, bai_ref, sp_ref,
                       y_ref, tail_ref, hl_ref, xbuf, sa, sb, hcar):
    tt = pl.program_id(1)
    t_blk = x_ref.shape[0]

    @pl.when(tt == 0)
    def _():
        xbuf[0:SUBLANES, :] = jnp.zeros((SUBLANES, LRU_WIDTH), F32)
        hcar[...] = jnp.zeros((1, LRU_WIDTH), F32)

    x = x_ref[...]
    xbuf[SUBLANES:SUBLANES + t_blk, :] = x
    base = SUBLANES - (CONV_WIDTH - 1)
    xc = cb_ref[...] + xbuf[base:base + t_blk, :] * cw_ref[0:1, :]
    for k in range(1, CONV_WIDTH):
        xc = xc + xbuf[base + k:base + k + t_blk, :] * cw_ref[k:k + 1, :]
    a_parts, b_parts = _lru_gates(xc, wai_ref, bai_ref, sp_ref)
    for n in range(LRU_BLOCKS):
        cs = slice(n * LRU_BLOCK, (n + 1) * LRU_BLOCK)
        sa[:, cs] = a_parts[n]
        sb[:, cs] = b_parts[n]

    row = lax.broadcasted_iota(jnp.int32, (SUBLANES, LRU_WIDTH), 0)

    def group(gi, h_prev):
        r0 = pl.multiple_of(gi * SUBLANES, SUBLANES)
        a = sa[pl.ds(r0, SUBLANES), :]
        b = sb[pl.ds(r0, SUBLANES), :]
        for s in (1, 2, 4):
            a_up = pltpu.roll(a, s, axis=0)
            b_up = pltpu.roll(b, s, axis=0)
            keep = row >= s
            b = jnp.where(keep, a * b_up + b, b)
            a = jnp.where(keep, a * a_up, a)
        h = a * h_prev + b
        sb[pl.ds(r0, SUBLANES), :] = h
        return h[SUBLANES - 1:SUBLANES, :]

    h_last = lax.fori_loop(0, t_blk // SUBLANES, group, hcar[...])
    hcar[...] = h_last
    h = sb[...]
    y_ref[...] = (h * jax.nn.gelu(gate_ref[...])).astype(y_ref.dtype)
    xbuf[0:SUBLANES, :] = x[t_blk - SUBLANES:, :]
    tail_ref[0] = x[t_blk - SUBLANES:, :]
    hl_ref[0] = h[t_blk - SUBLANES:, :]


def lru_prompt(z, batch, seq, cw, cb, wai, bai, sp, t_blk):
    nt = seq // t_blk
    w = LRU_WIDTH
    full = lambda shape: pl.BlockSpec(shape, lambda b, t: (0,) * len(shape))
    return pl.pallas_call(
        _lru_prompt_kernel,
        grid=(batch, nt),
        in_specs=[
            pl.BlockSpec((t_blk, w), lambda b, t: (b * nt + t, O_LRU_X // w)),
            pl.BlockSpec((t_blk, w), lambda b, t: (b * nt + t, O_LRU_G // w)),
            full((CONV_WIDTH, w)), full((1, w)), full((LRU_BLOCKS, LRU_BLOCK, 2 * LRU_BLOCK)),
            full((LRU_BLOCKS, 1, 2 * LRU_BLOCK)), full((1, w)),
        ],
        out_specs=[
            pl.BlockSpec((t_blk, w), lambda b, t: (b * nt + t, 0)),
            pl.BlockSpec((1, SUBLANES, w), lambda b, t: (b, 0, 0)),
            pl.BlockSpec((1, SUBLANES, w), lambda b, t: (b, 0, 0)),
        ],
        out_shape=[
            jax.ShapeDtypeStruct((batch * seq, w), BF16),
            jax.ShapeDtypeStruct((batch, SUBLANES, w), F32),
            jax.ShapeDtypeStruct((batch, SUBLANES, w), F32),
        ],
        scratch_shapes=[
            pltpu.VMEM((t_blk + SUBLANES, w), F32), pltpu.VMEM((t_blk, w), F32),
            pltpu.VMEM((t_blk, w), F32), pltpu.VMEM((1, w), F32),
        ],
        compiler_params=_params("parallel", "arbitrary"),
        name="lru_prompt",
    )(z, z, cw, cb, wai, bai, sp)


def _lru_sample_kernel(x_ref, gate_ref, c0_ref, c1_ref, c2_ref, h0_ref, cw_ref, cb_ref, wai_ref, bai_ref, sp_ref,
                       y_ref, h_ref):
    x = x_ref[...]
    xc = cb_ref[...] + c0_ref[...] * cw_ref[0:1, :]
    xc = xc + c1_ref[...] * cw_ref[1:2, :]
    xc = xc + c2_ref[...] * cw_ref[2:3, :]
    xc = xc + x * cw_ref[3:4, :]
    a_parts, b_parts = _lru_gates(xc, wai_ref, bai_ref, sp_ref)
    for n in range(LRU_BLOCKS):
        cs = slice(n * LRU_BLOCK, (n + 1) * LRU_BLOCK)
        h = a_parts[n] * h0_ref[:, cs] + b_parts[n]
        h_ref[:, cs] = h
        y_ref[:, cs] = (h * jax.nn.gelu(gate_ref[:, cs])).astype(y_ref.dtype)


def lru_sample(x, gate, c0, c1, c2, h0, cw, cb, wai, bai, sp):
    nb, w = x.shape
    return pl.pallas_call(
        _lru_sample_kernel,
        out_shape=[jax.ShapeDtypeStruct((nb, w), BF16), jax.ShapeDtypeStruct((nb, w), F32)],
        compiler_params=pltpu.CompilerParams(vmem_limit_bytes=V7X_VMEM_LIMIT_BYTES),
        name="lru_sample",
    )(x, gate, c0, c1, c2, h0, cw, cb, wai, bai, sp)


def _dil_prompt_kernel(q_ref, k_ref, v_ref, qg_ref, kg_ref, bias_ref, o_ref, lse_ref, kn_ref, qb, kb, vb):
    sub_len = q_ref.shape[0]
    kb[0:BAND, :] = jnp.zeros((BAND, DIL_OUT), BF16)
    vb[0:BAND, :] = jnp.zeros((BAND, DIL_OUT), BF16)
    for h in range(HPG):
        cs = slice(h * HEAD_DIM, (h + 1) * HEAD_DIM)
        qb[:, cs] = _rms(q_ref[:, cs], qg_ref[...]).astype(BF16)
        kn = _rms(k_ref[:, cs], kg_ref[...])
        kn_ref[:, cs] = kn
        kb[BAND:BAND + sub_len, cs] = kn.astype(BF16)
    vb[BAND:BAND + sub_len, :] = v_ref[...].astype(BF16)
    col = lax.broadcasted_iota(jnp.int32, (BAND, 2 * BAND), 1)

    def block(n, carry):
        r0 = pl.multiple_of(n * BAND, BAND)
        no_prev = jnp.logical_and(n == 0, col < BAND)
        for h in range(HPG):
            cs = slice(h * HEAD_DIM, (h + 1) * HEAD_DIM)
            q = qb[pl.ds(r0, BAND), cs]
            kk = kb[pl.ds(r0, 2 * BAND), cs]
            vv = vb[pl.ds(r0, 2 * BAND), cs]
            s = lax.dot_general(q, kk, (((1,), (1,)), ((), ())), preferred_element_type=F32)
            s = s * DIL_SCALE + bias_ref[h]
            s = jnp.where(no_prev, MASK_VALUE, s)
            m = jnp.max(s, axis=-1, keepdims=True)
            p = jnp.exp(s - m)
            den = jnp.sum(p, axis=-1, keepdims=True)
            o = jnp.dot(p.astype(BF16), vv, preferred_element_type=F32) / den
            o_ref[pl.ds(r0, BAND), cs] = o
            lse_ref[pl.ds(r0, BAND), cs] = jnp.broadcast_to(m + jnp.log(den), (BAND, HEAD_DIM))
        return carry

    lax.fori_loop(0, sub_len // BAND, block, 0)


def dil_prompt(z, batch, seq, g, qg, kg, bias):
    dil = DIL_PAIRS[g][1]
    sub_len = seq // dil
    rows = z.shape[0] // dil
    width = z.shape[1]
    zv = z.reshape(rows, dil * width)
    per_res = width // DIL_OUT
    spec = lambda off: pl.BlockSpec((sub_len, DIL_OUT), lambda b, r: (b, r * per_res + (off - O_Q) // DIL_OUT + g))
    full = lambda shape: pl.BlockSpec(shape, lambda b, r: (0,) * len(shape))
    out_spec = pl.BlockSpec((sub_len, DIL_OUT), lambda b, r: (b, r))
    out_sds = jax.ShapeDtypeStruct((batch * sub_len, dil * DIL_OUT), F32)
    o, lse, kn = pl.pallas_call(
        _dil_prompt_kernel,
        grid=(batch, dil),
        in_specs=[spec(O_Q), spec(O_K), spec(O_V), full((1, HEAD_DIM)), full((1, HEAD_DIM)),
                  full((HPG, BAND, 2 * BAND))],
        out_specs=[out_spec, out_spec, out_spec],
        out_shape=[out_sds, out_sds, out_sds],
        scratch_shapes=[pltpu.VMEM((sub_len, DIL_OUT), BF16), pltpu.VMEM((sub_len + BAND, DIL_OUT), BF16),
                        pltpu.VMEM((sub_len + BAND, DIL_OUT), BF16)],
        compiler_params=_params("parallel", "arbitrary"),
        name=f"dil_prompt_g{g}",
    )(zv, zv, zv, qg, kg, bias)
    n = batch * seq
    return o.reshape(n, DIL_OUT), lse.reshape(n, DIL_OUT), kn.reshape(n, DIL_OUT)


def _combine_kernel(o0, o1, o2, l0, l1, l2, y_ref):
    a0, a1, a2 = l0[...], l1[...], l2[...]
    m = jnp.maximum(jnp.maximum(a0, a1), a2)
    e0, e1, e2 = jnp.exp(a0 - m), jnp.exp(a1 - m), jnp.exp(a2 - m)
    tot = e0 + e1 + e2
    y = (e0 / tot) * o0[...] + (e1 / tot) * o1[...] + (e2 / tot) * o2[...]
    y_ref[...] = y.astype(y_ref.dtype)


def combine_groups(outs, lses, bm):
    n, w = outs[0].shape
    spec = pl.BlockSpec((bm, w), lambda i: (i, 0))
    return pl.pallas_call(
        _combine_kernel,
        grid=(n // bm,),
        in_specs=[spec] * 6,
        out_specs=spec,
        out_shape=jax.ShapeDtypeStruct((n, w), BF16),
        compiler_params=_params("parallel"),
        name="combine_groups",
    )(*outs, *lses)


def _dil_sample_kernel(q_ref, k_ref, v_ref, c0_ref, c1_ref, c2_ref, qg_ref, kg_ref, bcol_ref, b0_ref,
                       y_ref, kn_ref):
    caches = (c0_ref, c1_ref, c2_ref)
    outs = [[None] * N_GROUPS for _ in range(HPG)]
    lses = [[None] * N_GROUPS for _ in range(HPG)]
    for g in range(N_GROUPS):
        c_ref = caches[g]
        for hh in range(HPG):
            h = g * HPG + hh
            cs = slice(h * HEAD_DIM, (h + 1) * HEAD_DIM)
            q = _bf16_round(_rms(q_ref[:, :, cs], qg_ref[...]))
            kn = _rms(k_ref[:, :, cs], kg_ref[...])
            kn_ref[:, :, cs] = kn
            k_new = _bf16_round(kn)
            v_new = _bf16_round(v_ref[:, :, cs])
            kc = _bf16_round(c_ref[:, :, hh * HEAD_DIM:(hh + 1) * HEAD_DIM])
            vc = _bf16_round(c_ref[:, :, DIL_OUT + hh * HEAD_DIM:DIL_OUT + (hh + 1) * HEAD_DIM])
            s = jnp.sum(kc * q, axis=-1, keepdims=True) * DIL_SCALE + bcol_ref[h]
            s0 = jnp.sum(k_new * q, axis=-1, keepdims=True) * DIL_SCALE + b0_ref[h]
            m = jnp.maximum(jnp.max(s, axis=1, keepdims=True), s0)
            p = jnp.exp(s - m)
            p0 = jnp.exp(s0 - m)
            den = jnp.sum(p, axis=1, keepdims=True) + p0
            acc = jnp.sum(_bf16_round(p) * vc, axis=1, keepdims=True) + _bf16_round(p0) * v_new
            outs[hh][g] = acc / den
            lses[hh][g] = m + jnp.log(den)
    for hh in range(HPG):
        l0, l1, l2 = lses[hh]
        m = jnp.maximum(jnp.maximum(l0, l1), l2)
        e0, e1, e2 = jnp.exp(l0 - m), jnp.exp(l1 - m), jnp.exp(l2 - m)
        tot = e0 + e1 + e2
        y = (e0 / tot) * outs[hh][0] + (e1 / tot) * outs[hh][1] + (e2 / tot) * outs[hh][2]
        y_ref[:, :, hh * HEAD_DIM:(hh + 1) * HEAD_DIM] = y.astype(y_ref.dtype)


def dil_sample(zs3, caches, qg, kg, bcol, b0, bb):
    nb = zs3.shape[0]
    zspec = lambda off: pl.BlockSpec((bb, 1, DIL_WIDTH), lambda i: (i, 0, off // DIL_WIDTH))
    full = lambda shape: pl.BlockSpec(shape, lambda i: (0,) * len(shape))
    cviews, cspecs = [], []
    for g in range(N_GROUPS):
        dil = DIL_PAIRS[g][1]
        cviews.append(caches[g].reshape(nb, BAND, dil * 2 * DIL_OUT))
        cspecs.append(pl.BlockSpec((bb, BAND, 2 * DIL_OUT), lambda i: (i, 0, 0)))
    return pl.pallas_call(
        _dil_sample_kernel,
        grid=(nb // bb,),
        in_specs=[zspec(O_Q), zspec(O_K), zspec(O_V), *cspecs, full((1, 1, HEAD_DIM)), full((1, 1, HEAD_DIM)),
                  full((DIL_HEADS, BAND, 1)), full((DIL_HEADS, 1, 1))],
        out_specs=[pl.BlockSpec((bb, 1, DIL_OUT), lambda i: (i, 0, 0)),
                   pl.BlockSpec((bb, 1, DIL_WIDTH), lambda i: (i, 0, 0))],
        out_shape=[jax.ShapeDtypeStruct((nb, 1, DIL_OUT), BF16), jax.ShapeDtypeStruct((nb, 1, DIL_WIDTH), F32)],
        compiler_params=_params("parallel"),
        name="dil_sample",
    )(zs3, zs3, zs3, *cviews, qg, kg, bcol, b0)


def _mem_knorm_kernel(kv_ref, g_ref, o_ref):
    for h in range(MEM_HEADS):
        cs = slice(h * MEM_HEAD_DIM, (h + 1) * MEM_HEAD_DIM)
        o_ref[:, cs] = _rms(kv_ref[:, cs], g_ref[...])
    o_ref[:, MEM_WIDTH:] = kv_ref[:, MEM_WIDTH:]


def mem_knorm(kv, g, bm):
    m, w = kv.shape
    return pl.pallas_call(
        _mem_knorm_kernel,
        grid=(m // bm,),
        in_specs=[pl.BlockSpec((bm, w), lambda i: (i, 0)), pl.BlockSpec((1, MEM_HEAD_DIM), lambda i: (0, 0))],
        out_specs=pl.BlockSpec((bm, w), lambda i: (i, 0)),
        out_shape=jax.ShapeDtypeStruct((m, w), F32),
        compiler_params=_params("parallel"),
        name="mem_knorm",
    )(kv, g)


def _mem_prompt_kernel(qa_ref, qb_ref, kv_ref, g_ref, y_ref):
    for h in range(MEM_HEADS):
        q_ref = qa_ref if h < 2 else qb_ref
        qs = slice((h % 2) * MEM_HEAD_DIM, (h % 2 + 1) * MEM_HEAD_DIM)
        cs = slice(h * MEM_HEAD_DIM, (h + 1) * MEM_HEAD_DIM)
        q = _rms(q_ref[:, qs], g_ref[...]).astype(BF16)
        k = kv_ref[:, cs].astype(BF16)
        v = kv_ref[:, MEM_WIDTH + h * MEM_HEAD_DIM:MEM_WIDTH + (h + 1) * MEM_HEAD_DIM].astype(BF16)
        s = lax.dot_general(q, k, (((1,), (1,)), ((), ())), preferred_element_type=F32) * MEM_SCALE
        m = jnp.max(s, axis=-1, keepdims=True)
        e = jnp.exp(s - m)
        p = e / jnp.sum(e, axis=-1, keepdims=True)
        y_ref[:, cs] = jnp.dot(p.astype(BF16), v, preferred_element_type=F32).astype(y_ref.dtype)


def mem_attend_prompt(z, mkv, batch, seq, g, tq):
    nq = seq // tq
    mt = mkv.shape[0] // batch
    half = 2 * MEM_HEAD_DIM
    return pl.pallas_call(
        _mem_prompt_kernel,
        grid=(batch, nq),
        in_specs=[pl.BlockSpec((tq, half), lambda b, t: (b * nq + t, O_MQ // half)),
                  pl.BlockSpec((tq, half), lambda b, t: (b * nq + t, O_MQ // half + 1)),
                  pl.BlockSpec((mt, 2 * MEM_WIDTH), lambda b, t: (b, 0)),
                  pl.BlockSpec((1, MEM_HEAD_DIM), lambda b, t: (0, 0))],
        out_specs=pl.BlockSpec((tq, MEM_WIDTH), lambda b, t: (b * nq + t, 0)),
        out_shape=jax.ShapeDtypeStruct((batch * seq, MEM_WIDTH), BF16),
        compiler_params=_params("parallel", "arbitrary"),
        name="mem_prompt",
    )(z, z, mkv, g)


def _mem_sample_kernel(qa_ref, qb_ref, kv_ref, g_ref, y_ref):
    for h in range(MEM_HEADS):
        q_ref = qa_ref if h < 2 else qb_ref
        qs = slice((h % 2) * MEM_HEAD_DIM, (h % 2 + 1) * MEM_HEAD_DIM)
        cs = slice(h * MEM_HEAD_DIM, (h + 1) * MEM_HEAD_DIM)
        q = _bf16_round(_rms(q_ref[:, :, qs], g_ref[...]))
        k = _bf16_round(kv_ref[:, :, cs])
        v = _bf16_round(kv_ref[:, :, MEM_WIDTH + h * MEM_HEAD_DIM:MEM_WIDTH + (h + 1) * MEM_HEAD_DIM])
        s = jnp.sum(k * q, axis=-1, keepdims=True) * MEM_SCALE
        m = jnp.max(s, axis=1, keepdims=True)
        e = jnp.exp(s - m)
        p = e / jnp.sum(e, axis=1, keepdims=True)
        y_ref[:, :, cs] = jnp.sum(_bf16_round(p) * v, axis=1, keepdims=True).astype(y_ref.dtype)


def mem_sample(zs3, cache_mem, g, bb):
    nb = zs3.shape[0]
    mt = cache_mem.shape[1]
    half = 2 * MEM_HEAD_DIM
    return pl.pallas_call(
        _mem_sample_kernel,
        grid=(nb // bb,),
        in_specs=[pl.BlockSpec((bb, 1, half), lambda i: (i, 0, O_MQ // half)),
                  pl.BlockSpec((bb, 1, half), lambda i: (i, 0, O_MQ // half + 1)),
                  pl.BlockSpec((bb, mt, 2 * MEM_WIDTH), lambda i: (i, 0, 0)),
                  pl.BlockSpec((1, 1, MEM_HEAD_DIM), lambda i: (0, 0, 0))],
        out_specs=pl.BlockSpec((bb, 1, MEM_WIDTH), lambda i: (i, 0, 0)),
        out_shape=jax.ShapeDtypeStruct((nb, 1, MEM_WIDTH), BF16),
        compiler_params=_params("parallel"),
        name="mem_sample",
    )(zs3, zs3, cache_mem.reshape(nb, mt, 2 * MEM_WIDTH), g)


def _merge_kernel(ya_ref, yb_ref, ym_ref, wa_ref, wb_ref, wm_ref, g0_ref, g1_ref, g2_ref, o_ref):
    ma = jnp.dot(ya_ref[...], wa_ref[...], preferred_element_type=F32)
    mb = jnp.dot(yb_ref[...], wb_ref[...], preferred_element_type=F32)
    mm = jnp.dot(ym_ref[...], wm_ref[...], preferred_element_type=F32)
    merged = jax.nn.sigmoid(g0_ref[...]) * ma + jax.nn.sigmoid(g1_ref[...]) * mb + jax.nn.sigmoid(g2_ref[...]) * mm
    o_ref[...] = merged.astype(o_ref.dtype)


def merge(ya, yb, ym, wa, wb, wm, z, bm, bn):
    m = ya.shape[0]
    nj = D_MODEL // bn
    gate = lambda g: pl.BlockSpec((bm, bn), lambda i, j: (i, O_GATE // bn + g * nj + j))
    row = lambda k: pl.BlockSpec((bm, k), lambda i, j: (i, 0))
    col = lambda k: pl.BlockSpec((k, bn), lambda i, j: (0, j))
    return pl.pallas_call(
        _merge_kernel,
        grid=(m // bm, nj),
        in_specs=[row(LRU_WIDTH), row(DIL_OUT), row(MEM_WIDTH), col(LRU_WIDTH), col(DIL_OUT), col(MEM_WIDTH),
                  gate(0), gate(1), gate(2)],
        out_specs=pl.BlockSpec((bm, bn), lambda i, j: (i, j)),
        out_shape=jax.ShapeDtypeStruct((m, D_MODEL), BF16),
        compiler_params=_params("parallel", "arbitrary"),
        name="merge",
    )(ya, yb, ym, wa, wb, wm, z, z, z)


PEER_RANKS = PEER_TOPK + 1


def _candidate_pairs():
    return [(a, b) for a in range(PEER_RANKS) for b in range(PEER_RANKS) if (a + 1) * (b + 1) <= PEER_RANKS]


PEER_PAIRS = _candidate_pairs()
PEER_CAND_ROWS = -(-len(PEER_PAIRS) // SUBLANES) * SUBLANES


def _extract_top(c, count):
    rows = c.shape[0]
    idx = lax.broadcasted_iota(jnp.int32, c.shape, 0)
    out = []
    for _ in range(count):
        m = jnp.max(c, axis=0, keepdims=True)
        first = jnp.min(jnp.where(c == m, idx, rows), axis=0, keepdims=True)
        c = jnp.where(idx == first, -jnp.inf, c)
        out.append(m)
    return out


def _peer_stats_kernel(q_ref, sk_ref, s2_ref, e2_ref, thr_ref, e1_ref, cand):
    tm = q_ref.shape[0]
    half = PEER_DQ // 2
    for h in range(PEER_HEADS):
        st = []
        for p in range(2):
            c0 = (h * 2 + p) * half
            qhp = q_ref[:, c0:c0 + half].astype(BF16)
            st.append(lax.dot_general(sk_ref[h * 2 + p], qhp, (((1,), (1,)), ((), ())),
                                      preferred_element_type=F32))
        s1, s2 = st
        v1 = _extract_top(s1, PEER_RANKS)
        v2 = _extract_top(s2, PEER_RANKS)
        cand[...] = jnp.full((PEER_CAND_ROWS, tm), -jnp.inf, F32)
        for r, (a, b) in enumerate(PEER_PAIRS):
            cand[r:r + 1, :] = v1[a] + v2[b]
        top = _extract_top(cand[...], PEER_RANKS)
        zsum = jnp.ones_like(top[0])
        for r in range(1, PEER_TOPK):
            zsum = zsum + jnp.exp(top[r] - top[0])
        tau = 0.5 * (top[PEER_TOPK - 1] + top[PEER_TOPK])
        s2_ref[h] = s2
        e2_ref[h] = jnp.exp(s2 - v2[0]) / zsum
        thr_ref[h] = tau - s1
        e1_ref[h] = jnp.exp(s1 - v1[0])


def peer_stats(q, sk, tm):
    m = q.shape[0]
    out_sds = jax.ShapeDtypeStruct((PEER_HEADS, PEER_NKEYS, m), F32)
    out_spec = pl.BlockSpec((PEER_HEADS, PEER_NKEYS, tm), lambda i: (0, 0, i))
    return pl.pallas_call(
        _peer_stats_kernel,
        grid=(m // tm,),
        in_specs=[pl.BlockSpec((tm, PEER_HEADS * PEER_DQ), lambda i: (i, 0)),
                  pl.BlockSpec((2 * PEER_HEADS, PEER_NKEYS, PEER_DQ // 2), lambda i: (0, 0, 0))],
        out_specs=[out_spec] * 4,
        out_shape=[out_sds] * 4,
        scratch_shapes=[pltpu.VMEM((PEER_CAND_ROWS, tm), F32)],
        compiler_params=_params("parallel"),
        name="peer_stats",
    )(q, sk)


def _peer_dense_kernel(h_ref, u_ref, v_ref, s2_ref, e2_ref, thr_ref, e1_ref, o_ref, *, tn):
    e = pl.program_id(1)
    tm = h_ref.shape[0]

    @pl.when(e == 0)
    def _():
        o_ref[...] = jnp.zeros(o_ref.shape, F32)

    nj = tn // PEER_NKEYS
    gates = []
    for j in range(nj):
        i1 = e * nj + j
        gt = jnp.zeros((PEER_NKEYS, tm), F32)
        for h in range(PEER_HEADS):
            thr = thr_ref[h, pl.ds(i1, 1), :]
            e1 = e1_ref[h, pl.ds(i1, 1), :]
            gt = gt + e1 * jnp.where(s2_ref[h] >= thr, e2_ref[h], 0.0)
        gates.append(gt.T)
    gate = gates[0] if nj == 1 else jnp.concatenate(gates, axis=1)
    act = lax.dot_general(h_ref[...], u_ref[...], (((1,), (1,)), ((), ())), preferred_element_type=F32)
    w = (gate * jax.nn.gelu(act)).astype(BF16)
    o_ref[...] += jnp.dot(w, v_ref[...], preferred_element_type=F32)


def peer_dense(h2, u, v, stats, tm, tn):
    m = h2.shape[0]
    stat_spec = pl.BlockSpec((PEER_HEADS, PEER_NKEYS, tm), lambda i, e: (0, 0, i))
    return pl.pallas_call(
        functools.partial(_peer_dense_kernel, tn=tn),
        grid=(m // tm, PEER_N // tn),
        in_specs=[pl.BlockSpec((tm, D_MODEL), lambda i, e: (i, 0)),
                  pl.BlockSpec((tn, D_MODEL), lambda i, e: (e, 0)),
                  pl.BlockSpec((tn, D_MODEL), lambda i, e: (e, 0)),
                  stat_spec, stat_spec, stat_spec, stat_spec],
        out_specs=pl.BlockSpec((tm, D_MODEL), lambda i, e: (i, 0)),
        out_shape=jax.ShapeDtypeStruct((m, D_MODEL), F32),
        compiler_params=_params("parallel", "arbitrary"),
        name="peer_dense",
    )(h2, u, v, *stats)


def _t5_bucket(dist):
    max_exact = N_BUCKETS // 2
    d = jnp.maximum(dist, 1).astype(F32)
    large = max_exact + (jnp.log(d / max_exact) / math.log(MAX_DISTANCE / max_exact)
                         * (N_BUCKETS - max_exact)).astype(jnp.int32)
    large = jnp.minimum(large, N_BUCKETS - 1)
    return jnp.where(dist < max_exact, dist, large)


def _bias_tables(rel_bias):
    qi = jnp.arange(BAND)[:, None]
    ki = jnp.arange(2 * BAND)[None, :]
    off = qi + BAND - ki
    valid = (off >= 0) & (off <= BAND)
    band, cols = [], []
    for g in range(N_GROUPS):
        dil = DIL_PAIRS[g][1]
        dist = jnp.arange(BAND + 1, dtype=jnp.int32) * dil
        bj = rel_bias[_t5_bucket(dist)][:, g * HPG:(g + 1) * HPG].astype(F32)
        tb = bj[jnp.clip(off, 0, BAND)].transpose(2, 0, 1)
        band.append(jnp.where(valid[None], tb, MASK_VALUE))
        cols.append(bj.T)
    return band, jnp.concatenate(cols, axis=0)


def kernel(x_prompt, x_sample, state_conv, state_rnn, cache_swa0, cache_swa1, cache_swa2, cache_mem, mem_prompt, rel_bias, norm1_g, w_in, conv_w, conv_b, lru_wa, lru_ba, lru_wi, lru_bi, lru_lambda, dil_qn_g, dil_kn_g, mem_norm_g, w_mem_kv, mem_qn_g, mem_kn_g, w_br_a, w_br_b, w_br_m, w_out, norm2_g, peer_wq, peer_subkeys, peer_u, peer_v):
    batch, seq, d = x_prompt.shape
    nb = x_sample.shape[0]
    n_p = batch * seq
    m_all = -(-(n_p + nb) // ROW_PAD) * ROW_PAD
    bm_big = m_all // 8 if (m_all // 8) % 16 == 0 else ROW_PAD
    caches = (cache_swa0[0], cache_swa1[0], cache_swa2[0])

    w_in_b = w_in[0].astype(BF16)
    wai = jnp.concatenate([lru_wa[0], lru_wi[0]], axis=-1).astype(BF16)
    bai = jnp.concatenate([lru_ba[0], lru_bi[0]], axis=-1).reshape(LRU_BLOCKS, 1, 2 * LRU_BLOCK)
    sp = jax.nn.softplus(-lru_lambda[0].astype(F32)).reshape(1, LRU_WIDTH)
    cw, cb = conv_w[0], conv_b[0].reshape(1, LRU_WIDTH)
    qg, kg = dil_qn_g[0].reshape(1, HEAD_DIM), dil_kn_g[0].reshape(1, HEAD_DIM)
    band_bias, bias_cols = _bias_tables(rel_bias)
    bcol = bias_cols[:, BAND:0:-1].reshape(DIL_HEADS, BAND, 1)
    b0 = bias_cols[:, 0].reshape(DIL_HEADS, 1, 1)

    x_all = jnp.concatenate([x_prompt.reshape(n_p, d), x_sample.reshape(nb, d),
                             jnp.zeros((m_all - n_p - nb, d), F32)], axis=0)
    n1 = rms_cast(x_all, norm1_g[0], ROW_PAD)
    z = matmul(n1, w_in_b, bm_big, 512, name="in_proj")
    zs = z[n_p:n_p + nb]
    zs3 = zs.reshape(nb, 1, IN_COLS)

    ya_p, tail, hl = lru_prompt(z, batch, seq, cw, cb, wai, bai, sp, 512)
    sc0 = state_conv[0]
    ya_s, h_s = lru_sample(zs[:, O_LRU_X:O_LRU_G], zs[:, O_LRU_G:O_Q], sc0[:, 0], sc0[:, 1], sc0[:, 2],
                           state_rnn[0], cw, cb, wai, bai, sp)

    outs, lses, kns = [], [], []
    zqkv = z[:, O_Q:O_MQ]
    for g in range(N_GROUPS):
        o, lse, kn = dil_prompt(zqkv, batch, seq, g, qg, kg, band_bias[g])
        outs.append(o)
        lses.append(lse)
        kns.append(kn)
    yb_p = combine_groups(outs, lses, 1024)
    yb_s, kn_s = dil_sample(zs3, caches, qg.reshape(1, 1, HEAD_DIM), kg.reshape(1, 1, HEAD_DIM), bcol, b0, 8)

    mt = mem_prompt.shape[1]
    nm = rms_cast(mem_prompt.reshape(batch * mt, d), mem_norm_g[0], 256)
    kv = matmul(nm, w_mem_kv[0].astype(BF16), 512, 512, name="mem_kv")
    mkv = mem_knorm(kv, mem_kn_g[0].reshape(1, MEM_HEAD_DIM), 256)
    ym_p = mem_attend_prompt(z, mkv, batch, seq, mem_qn_g[0].reshape(1, MEM_HEAD_DIM), 512)
    ym_s = mem_sample(zs3, cache_mem[0], mem_qn_g[0].reshape(1, 1, MEM_HEAD_DIM), 4)

    pad = m_all - n_p - nb
    cat = lambda p, s: jnp.concatenate([p, s.reshape(nb, -1), jnp.zeros((pad, p.shape[1]), p.dtype)], axis=0)
    merged = merge(cat(ya_p, ya_s), cat(yb_p, yb_s), cat(ym_p, ym_s), w_br_a[0].astype(BF16),
                   w_br_b[0].astype(BF16), w_br_m[0].astype(BF16), z, bm_big, 512)
    x1 = matmul(merged, w_out[0].astype(BF16), bm_big, 512, residual=x_all, name="out_proj")

    h2 = rms_cast(x1, norm2_g[0], ROW_PAD)
    q = matmul(h2, peer_wq[0].astype(BF16), bm_big, 512, name="peer_q")
    sk = peer_subkeys[0].reshape(2 * PEER_HEADS, PEER_NKEYS, PEER_DQ // 2).astype(BF16)
    stats = peer_stats(q, sk, ROW_PAD)
    pe = peer_dense(h2, peer_u[0].astype(BF16), peer_v[0].astype(BF16), stats, ROW_PAD, 256)

    y_prompt = (x1[:n_p] + pe[:n_p]).reshape(batch, seq, d)
    y_sample = (x1[n_p:n_p + nb] + pe[n_p:n_p + nb]).reshape(nb, 1, d)

    p_conv = tail[:, SUBLANES - (CONV_WIDTH - 1):][None]
    p_rnn = hl[:, SUBLANES - 1][None]
    p_swa, s_swa = [], []
    for g in range(N_GROUPS):
        w = min(DIL_PAIRS[g][0], seq)
        kn = kns[g].reshape(batch, seq, HPG, HEAD_DIM)[:, seq - w:]
        vv = z[:n_p, O_V + g * DIL_OUT:O_V + (g + 1) * DIL_OUT].reshape(batch, seq, HPG, HEAD_DIM)[:, seq - w:]
        p_swa.append(jnp.stack([kn, vv], axis=2)[None])
        k_new = kn_s[:, 0, g * DIL_OUT:(g + 1) * DIL_OUT].reshape(nb, 1, 1, HPG, HEAD_DIM)
        v_new = zs[:, O_V + g * DIL_OUT:O_V + (g + 1) * DIL_OUT].reshape(nb, 1, 1, HPG, HEAD_DIM)
        s_swa.append(jnp.concatenate([caches[g][:, 1:], jnp.concatenate([k_new, v_new], axis=2)], axis=1)[None])
    p_mem = mkv.reshape(batch, mt, 2, MEM_HEADS, MEM_HEAD_DIM)[None]
    s_conv = jnp.concatenate([sc0[:, 1:], zs[:, None, O_LRU_X:O_LRU_G]], axis=1)[None]
    s_rnn = h_s[None]
    return (y_prompt, y_sample, p_conv, p_rnn, p_swa[0], p_swa[1], p_swa[2], p_mem,
            s_conv, s_rnn, s_swa[0], s_swa[1], s_swa[2])
```

```python
import functools
import math

import jax
import jax.numpy as jnp
from jax import lax
from jax.experimental import pallas as pl
from jax.experimental.pallas import tpu as pltpu

F32 = jnp.float32
BF16 = jnp.bfloat16

D_MODEL = 4096
HEAD_DIM = 128
LRU_WIDTH = 1536
LRU_BLOCKS = 12
LRU_BLOCK = LRU_WIDTH // LRU_BLOCKS
CONV_WIDTH = 4
LRU_C = 8.0
DIL_PAIRS = ((128, 1), (512, 4), (2048, 16))
N_GROUPS = len(DIL_PAIRS)
HPG = 4
DIL_HEADS = N_GROUPS * HPG
DIL_WIDTH = DIL_HEADS * HEAD_DIM
DIL_OUT = HPG * HEAD_DIM
BAND = 128
DIL_SCALE = HEAD_DIM ** -0.5
N_BUCKETS = 32
MAX_DISTANCE = 2048
MEM_HEADS = 4
MEM_HEAD_DIM = 256
MEM_WIDTH = MEM_HEADS * MEM_HEAD_DIM
MEM_SCALE = MEM_HEAD_DIM ** -0.5
N_BRANCH = 3
PEER_HEADS = 8
PEER_NKEYS = 128
PEER_N = PEER_NKEYS * PEER_NKEYS
PEER_DQ = 256
PEER_TOPK = 16
EPS = 1e-6

O_LRU_X = 0
O_LRU_G = O_LRU_X + LRU_WIDTH
O_Q = O_LRU_G + LRU_WIDTH
O_K = O_Q + DIL_WIDTH
O_V = O_K + DIL_WIDTH
O_MQ = O_V + DIL_WIDTH
O_GATE = O_MQ + MEM_WIDTH
IN_COLS = O_GATE + N_BRANCH * D_MODEL

V7X_VMEM_LIMIT_BYTES = 58 * 1024 * 1024
LANES = 128
SUBLANES = 8
MASK_VALUE = -1e30
ROW_PAD = 512


def _params(*sem):
    return pltpu.CompilerParams(dimension_semantics=sem, vmem_limit_bytes=V7X_VMEM_LIMIT_BYTES)


def _rms(x, g):
    return x * lax.rsqrt(jnp.mean(x * x, axis=-1, keepdims=True) + EPS) * g


def _bf16_round(x):
    return x.astype(BF16).astype(F32)


def _rms_cast_kernel(x_ref, g_ref, o_ref):
    o_ref[...] = _rms(x_ref[...], g_ref[...]).astype(o_ref.dtype)


def rms_cast(x, g, bm):
    m, d = x.shape
    return pl.pallas_call(
        _rms_cast_kernel,
        grid=(m // bm,),
        in_specs=[pl.BlockSpec((bm, d), lambda i: (i, 0)), pl.BlockSpec((1, d), lambda i: (0, 0))],
        out_specs=pl.BlockSpec((bm, d), lambda i: (i, 0)),
        out_shape=jax.ShapeDtypeStruct((m, d), BF16),
        compiler_params=_params("parallel"),
        name="rms_cast",
    )(x, g.reshape(1, d))


def _mm_kernel(a_ref, w_ref, o_ref):
    o_ref[...] = jnp.dot(a_ref[...], w_ref[...], preferred_element_type=F32).astype(o_ref.dtype)


def _mm_res_kernel(a_ref, w_ref, r_ref, o_ref):
    o_ref[...] = r_ref[...] + jnp.dot(a_ref[...], w_ref[...], preferred_element_type=F32)


def matmul(a, w, bm, bn, residual=None, out_dtype=F32, name="matmul"):
    m, k = a.shape
    n = w.shape[1]
    in_specs = [pl.BlockSpec((bm, k), lambda i, j: (i, 0)), pl.BlockSpec((k, bn), lambda i, j: (0, j))]
    args = [a, w]
    body = _mm_kernel
    if residual is not None:
        in_specs.append(pl.BlockSpec((bm, bn), lambda i, j: (i, j)))
        args.append(residual)
        body = _mm_res_kernel
    return pl.pallas_call(
        body,
        grid=(m // bm, n // bn),
        in_specs=in_specs,
        out_specs=pl.BlockSpec((bm, bn), lambda i, j: (i, j)),
        out_shape=jax.ShapeDtypeStruct((m, n), out_dtype),
        compiler_params=_params("parallel", "arbitrary"),
        name=name,
    )(*args)


def _lru_gates(xc, wai_ref, bai_ref, sp_ref):
    a_parts, b_parts = [], []
    for n in range(LRU_BLOCKS):
        cs = slice(n * LRU_BLOCK, (n + 1) * LRU_BLOCK)
        xn = xc[:, cs]
        ri = jnp.dot(xn.astype(BF16), wai_ref[n], preferred_element_type=F32) + bai_ref[n]
        r = jax.nn.sigmoid(ri[:, :LRU_BLOCK])
        i = jax.nn.sigmoid(ri[:, LRU_BLOCK:])
        log_a = -LRU_C * r * sp_ref[:, cs]
        a_parts.append(jnp.exp(log_a))
        th = jnp.tanh(log_a)
        b_parts.append(jnp.sqrt(-2.0 * th / (1.0 - th)) * (i * xn))
    return a_parts, b_parts


def _lru_prompt_kernel(x_ref, gate_ref, cw_ref, cb_ref, wai_ref, bai_ref, sp_ref,
                       y_ref, tail_ref, hl_ref, xbuf, sa, sb, hcar):
    tt = pl.program_id(1)
    t_blk = x_ref.shape[0]

    @pl.when(tt == 0)
    def _():
        xbuf[0:SUBLANES, :] = jnp.zeros((SUBLANES, LRU_WIDTH), F32)
        hcar[...] = jnp.zeros((1, LRU_WIDTH), F32)

    x = x_ref[...]
    xbuf[SUBLANES:SUBLANES + t_blk, :] = x
    base = SUBLANES - (CONV_WIDTH - 1)
    xc = cb_ref[...] + xbuf[base:base + t_blk, :] * cw_ref[0:1, :]
    for k in range(1, CONV_WIDTH):
        xc = xc + xbuf[base + k:base + k + t_blk, :] * cw_ref[k:k + 1, :]
    a_parts, b_parts = _lru_gates(xc, wai_ref, bai_ref, sp_ref)
    for n in range(LRU_BLOCKS):
        cs = slice(n * LRU_BLOCK, (n + 1) * LRU_BLOCK)
        sa[:, cs] = a_parts[n]
        sb[:, cs] = b_parts[n]

    row = lax.broadcasted_iota(jnp.int32, (SUBLANES, LRU_WIDTH), 0)

    def group(gi, h_prev):
        r0 = pl.multiple_of(gi * SUBLANES, SUBLANES)
        a = sa[pl.ds(r0, SUBLANES), :]
        b = sb[pl.ds(r0, SUBLANES), :]
        for s in (1, 2, 4):
            a_up = pltpu.roll(a, s, axis=0)
            b_up = pltpu.roll(b, s, axis=0)
            keep = row >= s
            b = jnp.where(keep, a * b_up + b, b)
            a = jnp.where(keep, a * a_up, a)
        h = a * h_prev + b
        sb[pl.ds(r0, SUBLANES), :] = h
        return h[SUBLANES - 1:SUBLANES, :]

    h_last = lax.fori_loop(0, t_blk // SUBLANES, group, hcar[...])
    hcar[...] = h_last
    h = sb[...]
    y_ref[...] = (h * jax.nn.gelu(gate_ref[...])).astype(y_ref.dtype)
    xbuf[0:SUBLANES, :] = x[t_blk - SUBLANES:, :]
    tail_ref[0] = x[t_blk - SUBLANES:, :]
    hl_ref[0] = h[t_blk - SUBLANES:, :]


def lru_prompt(z, batch, seq, cw, cb, wai, bai, sp, t_blk):
    nt = seq // t_blk
    w = LRU_WIDTH
    full = lambda shape: pl.BlockSpec(shape, lambda b, t: (0,) * len(shape))
    return pl.pallas_call(
        _lru_prompt_kernel,
        grid=(batch, nt),
        in_specs=[
            pl.BlockSpec((t_blk, w), lambda b, t: (b * nt + t, O_LRU_X // w)),
            pl.BlockSpec((t_blk, w), lambda b, t: (b * nt + t, O_LRU_G // w)),
            full((CONV_WIDTH, w)), full((1, w)), full((LRU_BLOCKS, LRU_BLOCK, 2 * LRU_BLOCK)),
            full((LRU_BLOCKS, 1, 2 * LRU_BLOCK)), full((1, w)),
        ],
        out_specs=[
            pl.BlockSpec((t_blk, w), lambda b, t: (b * nt + t, 0)),
            pl.BlockSpec((1, SUBLANES, w), lambda b, t: (b, 0, 0)),
            pl.BlockSpec((1, SUBLANES, w), lambda b, t: (b, 0, 0)),
        ],
        out_shape=[
            jax.ShapeDtypeStruct((batch * seq, w), BF16),
            jax.ShapeDtypeStruct((batch, SUBLANES, w), F32),
            jax.ShapeDtypeStruct((batch, SUBLANES, w), F32),
        ],
        scratch_shapes=[
            pltpu.VMEM((t_blk + SUBLANES, w), F32), pltpu.VMEM((t_blk, w), F32),
            pltpu.VMEM((t_blk, w), F32), pltpu.VMEM((1, w), F32),
        ],
        compiler_params=_params("parallel", "arbitrary"),
        name="lru_prompt",
    )(z, z, cw, cb, wai, bai, sp)


def _lru_sample_kernel(x_ref, gate_ref, c0_ref, c1_ref, c2_ref, h0_ref, cw_ref, cb_ref, wai_ref, bai_ref, sp_ref,
                       y_ref, h_ref):
    x = x_ref[...]
    xc = cb_ref[...] + c0_ref[...] * cw_ref[0:1, :]
    xc = xc + c1_ref[...] * cw_ref[1:2, :]
    xc = xc + c2_ref[...] * cw_ref[2:3, :]
    xc = xc + x * cw_ref[3:4, :]
    a_parts, b_parts = _lru_gates(xc, wai_ref, bai_ref, sp_ref)
    for n in range(LRU_BLOCKS):
        cs = slice(n * LRU_BLOCK, (n + 1) * LRU_BLOCK)
        h = a_parts[n] * h0_ref[:, cs] + b_parts[n]
        h_ref[:, cs] = h
        y_ref[:, cs] = (h * jax.nn.gelu(gate_ref[:, cs])).astype(y_ref.dtype)


def lru_sample(x, gate, c0, c1, c2, h0, cw, cb, wai, bai, sp):
    nb, w = x.shape
    return pl.pallas_call(
        _lru_sample_kernel,
        out_shape=[jax.ShapeDtypeStruct((nb, w), BF16), jax.ShapeDtypeStruct((nb, w), F32)],
        compiler_params=pltpu.CompilerParams(vmem_limit_bytes=V7X_VMEM_LIMIT_BYTES),
        name="lru_sample",
    )(x, gate, c0, c1, c2, h0, cw, cb, wai, bai, sp)


def _dil_prompt_kernel(q_ref, k_ref, v_ref, qg_ref, kg_ref, bias_ref, o_ref, lse_ref, kn_ref, qb, kb, vb):
    sub_len = q_ref.shape[0]
    kb[0:BAND, :] = jnp.zeros((BAND, DIL_OUT), BF16)
    vb[0:BAND, :] = jnp.zeros((BAND, DIL_OUT), BF16)
    for h in range(HPG):
        cs = slice(h * HEAD_DIM, (h + 1) * HEAD_DIM)
        qb[:, cs] = _rms(q_ref[:, cs], qg_ref[...]).astype(BF16)
        kn = _rms(k_ref[:, cs], kg_ref[...])
        kn_ref[:, cs] = kn
        kb[BAND:BAND + sub_len, cs] = kn.astype(BF16)
    vb[BAND:BAND + sub_len, :] = v_ref[...].astype(BF16)
    col = lax.broadcasted_iota(jnp.int32, (BAND, 2 * BAND), 1)

    def block(n, carry):
        r0 = pl.multiple_of(n * BAND, BAND)
        no_prev = jnp.logical_and(n == 0, col < BAND)
        for h in range(HPG):
            cs = slice(h * HEAD_DIM, (h + 1) * HEAD_DIM)
            q = qb[pl.ds(r0, BAND), cs]
            kk = kb[pl.ds(r0, 2 * BAND), cs]
            vv = vb[pl.ds(r0, 2 * BAND), cs]
            s = lax.dot_general(q, kk, (((1,), (1,)), ((), ())), preferred_element_type=F32)
            s = s * DIL_SCALE + bias_ref[h]
            s = jnp.where(no_prev, MASK_VALUE, s)
            m = jnp.max(s, axis=-1, keepdims=True)
            p = jnp.exp(s - m)
            den = jnp.sum(p, axis=-1, keepdims=True)
            o = jnp.dot(p.astype(BF16), vv, preferred_element_type=F32) / den
            o_ref[pl.ds(r0, BAND), cs] = o
            lse_ref[pl.ds(r0, BAND), cs] = jnp.broadcast_to(m + jnp.log(den), (BAND, HEAD_DIM))
        return carry

    lax.fori_loop(0, sub_len // BAND, block, 0)


def dil_prompt(z, batch, seq, g, qg, kg, bias):
    dil = DIL_PAIRS[g][1]
    sub_len = seq // dil
    rows = z.shape[0] // dil
    width = z.shape[1]
    zv = z.reshape(rows, dil * width)
    per_res = width // DIL_OUT
    spec = lambda off: pl.BlockSpec((sub_len, DIL_OUT), lambda b, r: (b, r * per_res + (off - O_Q) // DIL_OUT + g))
    full = lambda shape: pl.BlockSpec(shape, lambda b, r: (0,) * len(shape))
    out_spec = pl.BlockSpec((sub_len, DIL_OUT), lambda b, r: (b, r))
    out_sds = jax.ShapeDtypeStruct((batch * sub_len, dil * DIL_OUT), F32)
    o, lse, kn = pl.pallas_call(
        _dil_prompt_kernel,
        grid=(batch, dil),
        in_specs=[spec(O_Q), spec(O_K), spec(O_V), full((1, HEAD_DIM)), full((1, HEAD_DIM)),
                  full((HPG, BAND, 2 * BAND))],
        out_specs=[out_spec, out_spec, out_spec],
        out_shape=[out_sds, out_sds, out_sds],
        scratch_shapes=[pltpu.VMEM((sub_len, DIL_OUT), BF16), pltpu.VMEM((sub_len + BAND, DIL_OUT), BF16),
                        pltpu.VMEM((sub_len + BAND, DIL_OUT), BF16)],
        compiler_params=_params("parallel", "arbitrary"),
        name=f"dil_prompt_g{g}",
    )(zv, zv, zv, qg, kg, bias)
    n = batch * seq
    return o.reshape(n, DIL_OUT), lse.reshape(n, DIL_OUT), kn.reshape(n, DIL_OUT)


def _combine_kernel(o0, o1, o2, l0, l1, l2, y_ref):
    a0, a1, a2 = l0[...], l1[...], l2[...]
    m = jnp.maximum(jnp.maximum(a0, a1), a2)
    e0, e1, e2 = jnp.exp(a0 - m), jnp.exp(a1 - m), jnp.exp(a2 - m)
    tot = e0 + e1 + e2
    y = (e0 / tot) * o0[...] + (e1 / tot) * o1[...] + (e2 / tot) * o2[...]
    y_ref[...] = y.astype(y_ref.dtype)


def combine_groups(outs, lses, bm):
    n, w = outs[0].shape
    spec = pl.BlockSpec((bm, w), lambda i: (i, 0))
    return pl.pallas_call(
        _combine_kernel,
        grid=(n // bm,),
        in_specs=[spec] * 6,
        out_specs=spec,
        out_shape=jax.ShapeDtypeStruct((n, w), BF16),
        compiler_params=_params("parallel"),
        name="combine_groups",
    )(*outs, *lses)


def _dil_sample_kernel(q_ref, k_ref, v_ref, c0_ref, c1_ref, c2_ref, qg_ref, kg_ref, bcol_ref, b0_ref,
                       y_ref, kn_ref):
    caches = (c0_ref, c1_ref, c2_ref)
    outs = [[None] * N_GROUPS for _ in range(HPG)]
    lses = [[None] * N_GROUPS for _ in range(HPG)]
    for g in range(N_GROUPS):
        c_ref = caches[g]
        for hh in range(HPG):
            h = g * HPG + hh
            cs = slice(h * HEAD_DIM, (h + 1) * HEAD_DIM)
            q = _bf16_round(_rms(q_ref[:, :, cs], qg_ref[...]))
            kn = _rms(k_ref[:, :, cs], kg_ref[...])
            kn_ref[:, :, cs] = kn
            k_new = _bf16_round(kn)
            v_new = _bf16_round(v_ref[:, :, cs])
            kc = _bf16_round(c_ref[:, :, hh * HEAD_DIM:(hh + 1) * HEAD_DIM])
            vc = _bf16_round(c_ref[:, :, DIL_OUT + hh * HEAD_DIM:DIL_OUT + (hh + 1) * HEAD_DIM])
            s = jnp.sum(kc * q, axis=-1, keepdims=True) * DIL_SCALE + bcol_ref[h]
            s0 = jnp.sum(k_new * q, axis=-1, keepdims=True) * DIL_SCALE + b0_ref[h]
            m = jnp.maximum(jnp.max(s, axis=1, keepdims=True), s0)
            p = jnp.exp(s - m)
            p0 = jnp.exp(s0 - m)
            den = jnp.sum(p, axis=1, keepdims=True) + p0
            acc = jnp.sum(_bf16_round(p) * vc, axis=1, keepdims=True) + _bf16_round(p0) * v_new
            outs[hh][g] = acc / den
            lses[hh][g] = m + jnp.log(den)
    for hh in range(HPG):
        l0, l1, l2 = lses[hh]
        m = jnp.maximum(jnp.maximum(l0, l1), l2)
        e0, e1, e2 = jnp.exp(l0 - m), jnp.exp(l1 - m), jnp.exp(l2 - m)
        tot = e0 + e1 + e2
        y = (e0 / tot) * outs[hh][0] + (e1 / tot) * outs[hh][1] + (e2 / tot) * outs[hh][2]
        y_ref[:, :, hh * HEAD_DIM:(hh + 1) * HEAD_DIM] = y.astype(y_ref.dtype)


def dil_sample(zs3, caches, qg, kg, bcol, b0, bb):
    nb = zs3.shape[0]
    zspec = lambda off: pl.BlockSpec((bb, 1, DIL_WIDTH), lambda i: (i, 0, off // DIL_WIDTH))
    full = lambda shape: pl.BlockSpec(shape, lambda i: (0,) * len(shape))
    cviews, cspecs = [], []
    for g in range(N_GROUPS):
        dil = DIL_PAIRS[g][1]
        cviews.append(caches[g].reshape(nb, BAND, dil * 2 * DIL_OUT))
        cspecs.append(pl.BlockSpec((bb, BAND, 2 * DIL_OUT), lambda i: (i, 0, 0)))
    return pl.pallas_call(
        _dil_sample_kernel,
        grid=(nb // bb,),
        in_specs=[zspec(O_Q), zspec(O_K), zspec(O_V), *cspecs, full((1, 1, HEAD_DIM)), full((1, 1, HEAD_DIM)),
                  full((DIL_HEADS, BAND, 1)), full((DIL_HEADS, 1, 1))],
        out_specs=[pl.BlockSpec((bb, 1, DIL_OUT), lambda i: (i, 0, 0)),
                   pl.BlockSpec((bb, 1, DIL_WIDTH), lambda i: (i, 0, 0))],
        out_shape=[jax.ShapeDtypeStruct((nb, 1, DIL_OUT), BF16), jax.ShapeDtypeStruct((nb, 1, DIL_WIDTH), F32)],
        compiler_params=_params("parallel"),
        name="dil_sample",
    )(zs3, zs3, zs3, *cviews, qg, kg, bcol, b0)


def _mem_knorm_kernel(kv_ref, g_ref, o_ref):
    for h in range(MEM_HEADS):
        cs = slice(h * MEM_HEAD_DIM, (h + 1) * MEM_HEAD_DIM)
        o_ref[:, cs] = _rms(kv_ref[:, cs], g_ref[...])
    o_ref[:, MEM_WIDTH:] = kv_ref[:, MEM_WIDTH:]


def mem_knorm(kv, g, bm):
    m, w = kv.shape
    return pl.pallas_call(
        _mem_knorm_kernel,
        grid=(m // bm,),
        in_specs=[pl.BlockSpec((bm, w), lambda i: (i, 0)), pl.BlockSpec((1, MEM_HEAD_DIM), lambda i: (0, 0))],
        out_specs=pl.BlockSpec((bm, w), lambda i: (i, 0)),
        out_shape=jax.ShapeDtypeStruct((m, w), F32),
        compiler_params=_params("parallel"),
        name="mem_knorm",
    )(kv, g)


def _mem_prompt_kernel(qa_ref, qb_ref, kv_ref, g_ref, y_ref):
    for h in range(MEM_HEADS):
        q_ref = qa_ref if h < 2 else qb_ref
        qs = slice((h % 2) * MEM_HEAD_DIM, (h % 2 + 1) * MEM_HEAD_DIM)
        cs = slice(h * MEM_HEAD_DIM, (h + 1) * MEM_HEAD_DIM)
        q = _rms(q_ref[:, qs], g_ref[...]).astype(BF16)
        k = kv_ref[:, cs].astype(BF16)
        v = kv_ref[:, MEM_WIDTH + h * MEM_HEAD_DIM:MEM_WIDTH + (h + 1) * MEM_HEAD_DIM].astype(BF16)
        s = lax.dot_general(q, k, (((1,), (1,)), ((), ())), preferred_element_type=F32) * MEM_SCALE
        m = jnp.max(s, axis=-1, keepdims=True)
        e = jnp.exp(s - m)
        p = e / jnp.sum(e, axis=-1, keepdims=True)
        y_ref[:, cs] = jnp.dot(p.astype(BF16), v, preferred_element_type=F32).astype(y_ref.dtype)


def mem_attend_prompt(z, mkv, batch, seq, g, tq):
    nq = seq // tq
    mt = mkv.shape[0] // batch
    half = 2 * MEM_HEAD_DIM
    return pl.pallas_call(
        _mem_prompt_kernel,
        grid=(batch, nq),
        in_specs=[pl.BlockSpec((tq, half), lambda b, t: (b * nq + t, O_MQ // half)),
                  pl.BlockSpec((tq, half), lambda b, t: (b * nq + t, O_MQ // half + 1)),
                  pl.BlockSpec((mt, 2 * MEM_WIDTH), lambda b, t: (b, 0)),
                  pl.BlockSpec((1, MEM_HEAD_DIM), lambda b, t: (0, 0))],
        out_specs=pl.BlockSpec((tq, MEM_WIDTH), lambda b, t: (b * nq + t, 0)),
        out_shape=jax.ShapeDtypeStruct((batch * seq, MEM_WIDTH), BF16),
        compiler_params=_params("parallel", "arbitrary"),
        name="mem_prompt",
    )(z, z, mkv, g)


def _mem_sample_kernel(qa_ref, qb_ref, kv_ref, g_ref, y_ref):
    for h in range(MEM_HEADS):
        q_ref = qa_ref if h < 2 else qb_ref
        qs = slice((h % 2) * MEM_HEAD_DIM, (h % 2 + 1) * MEM_HEAD_DIM)
        cs = slice(h * MEM_HEAD_DIM, (h + 1) * MEM_HEAD_DIM)
        q = _bf16_round(_rms(q_ref[:, :, qs], g_ref[...]))
        k = _bf16_round(kv_ref[:, :, cs])
        v = _bf16_round(kv_ref[:, :, MEM_WIDTH + h * MEM_HEAD_DIM:MEM_WIDTH + (h + 1) * MEM_HEAD_DIM])
        s = jnp.sum(k * q, axis=-1, keepdims=True) * MEM_SCALE
        m = jnp.max(s, axis=1, keepdims=True)
        e = jnp.exp(s - m)
        p = e / jnp.sum(e, axis=1, keepdims=True)
        y_ref[:, :, cs] = jnp.sum(_bf16_round(p) * v, axis=1, keepdims=True).astype(y_ref.dtype)


def mem_sample(zs3, cache_mem, g, bb):
    nb = zs3.shape[0]
    mt = cache_mem.shape[1]
    half = 2 * MEM_HEAD_DIM
    return pl.pallas_call(
        _mem_sample_kernel,
        grid=(nb // bb,),
        in_specs=[pl.BlockSpec((bb, 1, half), lambda i: (i, 0, O_MQ // half)),
                  pl.BlockSpec((bb, 1, half), lambda i: (i, 0, O_MQ // half + 1)),
                  pl.BlockSpec((bb, mt, 2 * MEM_WIDTH), lambda i: (i, 0, 0)),
                  pl.BlockSpec((1, 1, MEM_HEAD_DIM), lambda i: (0, 0, 0))],
        out_specs=pl.BlockSpec((bb, 1, MEM_WIDTH), lambda i: (i, 0, 0)),
        out_shape=jax.ShapeDtypeStruct((nb, 1, MEM_WIDTH), BF16),
        compiler_params=_params("parallel"),
        name="mem_sample",
    )(zs3, zs3, cache_mem.reshape(nb, mt, 2 * MEM_WIDTH), g)


def _merge_kernel(ya_ref, yb_ref, ym_ref, wa_ref, wb_ref, wm_ref, g0_ref, g1_ref, g2_ref, o_ref):
    ma = jnp.dot(ya_ref[...], wa_ref[...], preferred_element_type=F32)
    mb = jnp.dot(yb_ref[...], wb_ref[...], preferred_element_type=F32)
    mm = jnp.dot(ym_ref[...], wm_ref[...], preferred_element_type=F32)
    merged = jax.nn.sigmoid(g0_ref[...]) * ma + jax.nn.sigmoid(g1_ref[...]) * mb + jax.nn.sigmoid(g2_ref[...]) * mm
    o_ref[...] = merged.astype(o_ref.dtype)


def merge(ya, yb, ym, wa, wb, wm, z, bm, bn):
    m = ya.shape[0]
    nj = D_MODEL // bn
    gate = lambda g: pl.BlockSpec((bm, bn), lambda i, j: (i, O_GATE // bn + g * nj + j))
    row = lambda k: pl.BlockSpec((bm, k), lambda i, j: (i, 0))
    col = lambda k: pl.BlockSpec((k, bn), lambda i, j: (0, j))
    return pl.pallas_call(
        _merge_kernel,
        grid=(m // bm, nj),
        in_specs=[row(LRU_WIDTH), row(DIL_OUT), row(MEM_WIDTH), col(LRU_WIDTH), col(DIL_OUT), col(MEM_WIDTH),
                  gate(0), gate(1), gate(2)],
        out_specs=pl.BlockSpec((bm, bn), lambda i, j: (i, j)),
        out_shape=jax.ShapeDtypeStruct((m, D_MODEL), BF16),
        compiler_params=_params("parallel", "arbitrary"),
        name="merge",
    )(ya, yb, ym, wa, wb, wm, z, z, z)


PEER_RANKS = PEER_TOPK + 1


def _candidate_pairs():
    return [(a, b) for a in range(PEER_RANKS) for b in range(PEER_RANKS) if (a + 1) * (b + 1) <= PEER_RANKS]


PEER_PAIRS = _candidate_pairs()
PEER_CAND_ROWS = -(-len(PEER_PAIRS) // SUBLANES) * SUBLANES


def _extract_top(c, count):
    rows = c.shape[0]
    idx = lax.broadcasted_iota(jnp.int32, c.shape, 0)
    out = []
    for _ in range(count):
        m = jnp.max(c, axis=0, keepdims=True)
        first = jnp.min(jnp.where(c == m, idx, rows), axis=0, keepdims=True)
        c = jnp.where(idx == first, -jnp.inf, c)
        out.append(m)
    return out


def _peer_stats_kernel(q_ref, sk_ref, s2_ref, e2_ref, thr_ref, e1_ref, cand):
    tm = q_ref.shape[0]
    half = PEER_DQ // 2
    for h in range(PEER_HEADS):
        st = []
        for p in range(2):
            c0 = (h * 2 + p) * half
            qhp = q_ref[:, c0:c0 + half].astype(BF16)
            st.append(lax.dot_general(sk_ref[h * 2 + p], qhp, (((1,), (1,)), ((), ())),
                                      preferred_element_type=F32))
        s1, s2 = st
        v1 = _extract_top(s1, PEER_RANKS)
        v2 = _extract_top(s2, PEER_RANKS)
        cand[...] = jnp.full((PEER_CAND_ROWS, tm), -jnp.inf, F32)
        for r, (a, b) in enumerate(PEER_PAIRS):
            cand[r:r + 1, :] = v1[a] + v2[b]
        top = _extract_top(cand[...], PEER_RANKS)
        zsum = jnp.ones_like(top[0])
        for r in range(1, PEER_TOPK):
            zsum = zsum + jnp.exp(top[r] - top[0])
        tau = 0.5 * (top[PEER_TOPK - 1] + top[PEER_TOPK])
        s2_ref[h] = s2
        e2_ref[h] = jnp.exp(s2 - v2[0]) / zsum
        thr_ref[h] = tau - s1
        e1_ref[h] = jnp.exp(s1 - v1[0])


def peer_stats(q, sk, tm):
    m = q.shape[0]
    out_sds = jax.ShapeDtypeStruct((PEER_HEADS, PEER_NKEYS, m), F32)
    out_spec = pl.BlockSpec((PEER_HEADS, PEER_NKEYS, tm), lambda i: (0, 0, i))
    return pl.pallas_call(
        _peer_stats_kernel,
        grid=(m // tm,),
        in_specs=[pl.BlockSpec((tm, PEER_HEADS * PEER_DQ), lambda i: (i, 0)),
                  pl.BlockSpec((2 * PEER_HEADS, PEER_NKEYS, PEER_DQ // 2), lambda i: (0, 0, 0))],
        out_specs=[out_spec] * 4,
        out_shape=[out_sds] * 4,
        scratch_shapes=[pltpu.VMEM((PEER_CAND_ROWS, tm), F32)],
        compiler_params=_params("parallel"),
        name="peer_stats",
    )(q, sk)


def _peer_dense_kernel(h_ref, u_ref, v_ref, s2_ref, e2_ref, thr_ref, e1_ref, o_ref, *, tn):
    e = pl.program_id(1)
    tm = h_ref.shape[0]

    @pl.when(e == 0)
    def _():
        o_ref[...] = jnp.zeros(o_ref.shape, F32)

    nj = tn // PEER_NKEYS
    gates = []
    for j in range(nj):
        i1 = e * nj + j
        gt = jnp.zeros((PEER_NKEYS, tm), F32)
        for h in range(PEER_HEADS):
            thr = thr_ref[h, pl.ds(i1, 1), :]
            e1 = e1_ref[h, pl.ds(i1, 1), :]
            gt = gt + e1 * jnp.where(s2_ref[h] >= thr, e2_ref[h], 0.0)
        gates.append(gt.T)
    gate = gates[0] if nj == 1 else jnp.concatenate(gates, axis=1)
    act = lax.dot_general(h_ref[...], u_ref[...], (((1,), (1,)), ((), ())), preferred_element_type=F32)
    w = (gate * jax.nn.gelu(act)).astype(BF16)
    o_ref[...] += jnp.dot(w, v_ref[...], preferred_element_type=F32)


def peer_dense(h2, u, v, stats, tm, tn):
    m = h2.shape[0]
    stat_spec = pl.BlockSpec((PEER_HEADS, PEER_NKEYS, tm), lambda i, e: (0, 0, i))
    return pl.pallas_call(
        functools.partial(_peer_dense_kernel, tn=tn),
        grid=(m // tm, PEER_N // tn),
        in_specs=[pl.BlockSpec((tm, D_MODEL), lambda i, e: (i, 0)),
                  pl.BlockSpec((tn, D_MODEL), lambda i, e: (e, 0)),
                  pl.BlockSpec((tn, D_MODEL), lambda i, e: (e, 0)),
                  stat_spec, stat_spec, stat_spec, stat_spec],
        out_specs=pl.BlockSpec((tm, D_MODEL), lambda i, e: (i, 0)),
        out_shape=jax.ShapeDtypeStruct((m, D_MODEL), F32),
        compiler_params=_params("parallel", "arbitrary"),
        name="peer_dense",
    )(h2, u, v, *stats)


def _t5_bucket(dist):
    max_exact = N_BUCKETS // 2
    d = jnp.maximum(dist, 1).astype(F32)
    large = max_exact + (jnp.log(d / max_exact) / math.log(MAX_DISTANCE / max_exact)
                         * (N_BUCKETS - max_exact)).astype(jnp.int32)
    large = jnp.minimum(large, N_BUCKETS - 1)
    return jnp.where(dist < max_exact, dist, large)


def _bias_tables(rel_bias):
    qi = jnp.arange(BAND)[:, None]
    ki = jnp.arange(2 * BAND)[None, :]
    off = qi + BAND - ki
    valid = (off >= 0) & (off <= BAND)
    band, cols = [], []
    for g in range(N_GROUPS):
        dil = DIL_PAIRS[g][1]
        dist = jnp.arange(BAND + 1, dtype=jnp.int32) * dil
        bj = rel_bias[_t5_bucket(dist)][:, g * HPG:(g + 1) * HPG].astype(F32)
        tb = bj[jnp.clip(off, 0, BAND)].transpose(2, 0, 1)
        band.append(jnp.where(valid[None], tb, MASK_VALUE))
        cols.append(bj.T)
    return band, jnp.concatenate(cols, axis=0)


def kernel(x_prompt, x_sample, state_conv, state_rnn, cache_swa0, cache_swa1, cache_swa2, cache_mem, mem_prompt, rel_bias, norm1_g, w_in, conv_w, conv_b, lru_wa, lru_ba, lru_wi, lru_bi, lru_lambda, dil_qn_g, dil_kn_g, mem_norm_g, w_mem_kv, mem_qn_g, mem_kn_g, w_br_a, w_br_b, w_br_m, w_out, norm2_g, peer_wq, peer_subkeys, peer_u, peer_v):
    batch, seq, d = x_prompt.shape
    nb = x_sample.shape[0]
    n_p = batch * seq
    m_all = -(-(n_p + nb) // ROW_PAD) * ROW_PAD
    bm_big = m_all // 8 if (m_all // 8) % 16 == 0 else ROW_PAD
    caches = (cache_swa0[0], cache_swa1[0], cache_swa2[0])

    w_in_b = w_in[0].astype(BF16)
    wai = jnp.concatenate([lru_wa[0], lru_wi[0]], axis=-1).astype(BF16)
    bai = jnp.concatenate([lru_ba[0], lru_bi[0]], axis=-1).reshape(LRU_BLOCKS, 1, 2 * LRU_BLOCK)
    sp = jax.nn.softplus(-lru_lambda[0].astype(F32)).reshape(1, LRU_WIDTH)
    cw, cb = conv_w[0], conv_b[0].reshape(1, LRU_WIDTH)
    qg, kg = dil_qn_g[0].reshape(1, HEAD_DIM), dil_kn_g[0].reshape(1, HEAD_DIM)
    band_bias, bias_cols = _bias_tables(rel_bias)
    bcol = bias_cols[:, BAND:0:-1].reshape(DIL_HEADS, BAND, 1)
    b0 = bias_cols[:, 0].reshape(DIL_HEADS, 1, 1)

    x_all = jnp.concatenate([x_prompt.reshape(n_p, d), x_sample.reshape(nb, d),
                             jnp.zeros((m_all - n_p - nb, d), F32)], axis=0)
    n1 = rms_cast(x_all, norm1_g[0], ROW_PAD)
    z = matmul(n1, w_in_b, bm_big, 512, name="in_proj")
    zs = z[n_p:n_p + nb]
    zs3 = zs.reshape(nb, 1, IN_COLS)

    ya_p, tail, hl = lru_prompt(z, batch, seq, cw, cb, wai, bai, sp, 512)
    sc0 = state_conv[0]
    ya_s, h_s = lru_sample(zs[:, O_LRU_X:O_LRU_G], zs[:, O_LRU_G:O_Q], sc0[:, 0], sc0[:, 1], sc0[:, 2],
                           state_rnn[0], cw, cb, wai, bai, sp)

    outs, lses, kns = [], [], []
    zqkv = z[:, O_Q:O_MQ]
    for g in range(N_GROUPS):
        o, lse, kn = dil_prompt(zqkv, batch, seq, g, qg, kg, band_bias[g])
        outs.append(o)
        lses.append(lse)
        kns.append(kn)
    yb_p = combine_groups(outs, lses, 1024)
    yb_s, kn_s = dil_sample(zs3, caches, qg.reshape(1, 1, HEAD_DIM), kg.reshape(1, 1, HEAD_DIM), bcol, b0, 8)

    mt = mem_prompt.shape[1]
    nm = rms_cast(mem_prompt.reshape(batch * mt, d), mem_norm_g[0], 256)
    kv = matmul(nm, w_mem_kv[0].astype(BF16), 512, 512, name="mem_kv")
    mkv = mem_knorm(kv, mem_kn_g[0].reshape(1, MEM_HEAD_DIM), 256)
    ym_p = mem_attend_prompt(z, mkv, batch, seq, mem_qn_g[0].reshape(1, MEM_HEAD_DIM), 512)
    ym_s = mem_sample(zs3, cache_mem[0], mem_qn_g[0].reshape(1, 1, MEM_HEAD_DIM), 4)

    pad = m_all - n_p - nb
    cat = lambda p, s: jnp.concatenate([p, s.reshape(nb, -1), jnp.zeros((pad, p.shape[1]), p.dtype)], axis=0)
    merged = merge(cat(ya_p, ya_s), cat(yb_p, yb_s), cat(ym_p, ym_s), w_br_a[0].astype(BF16),
                   w_br_b[0].astype(BF16), w_br_m[0].astype(BF16), z, bm_big, 512)
    x1 = matmul(merged, w_out[0].astype(BF16), bm_big, 512, residual=x_all, name="out_proj")

    h2 = rms_cast(x1, norm2_g[0], ROW_PAD)
    q = matmul(h2, peer_wq[0].astype(BF16), bm_big, 512, name="peer_q")
    sk = peer_subkeys[0].reshape(2 * PEER_HEADS, PEER_NKEYS, PEER_DQ // 2).astype(BF16)
    stats = peer_stats(q, sk, ROW_PAD)
    pe = peer_dense(h2, peer_u[0].astype(BF16), peer_v[0].astype(BF16), stats, ROW_PAD, 512)

    y_prompt = (x1[:n_p] + pe[:n_p]).reshape(batch, seq, d)
    y_sample = (x1[n_p:n_p + nb] + pe[n_p:n_p + nb]).reshape(nb, 1, d)

    p_conv = tail[:, SUBLANES - (CONV_WIDTH - 1):][None]
    p_rnn = hl[:, SUBLANES - 1][None]
    p_swa, s_swa = [], []
    for g in range(N_GROUPS):
        w = min(DIL_PAIRS[g][0], seq)
        kn = kns[g].reshape(batch, seq, HPG, HEAD_DIM)[:, seq - w:]
        vv = z[:n_p, O_V + g * DIL_OUT:O_V + (g + 1) * DIL_OUT].reshape(batch, seq, HPG, HEAD_DIM)[:, seq - w:]
        p_swa.append(jnp.stack([kn, vv], axis=2)[None])
        k_new = kn_s[:, 0, g * DIL_OUT:(g + 1) * DIL_OUT].reshape(nb, 1, 1, HPG, HEAD_DIM)
        v_new = zs[:, O_V + g * DIL_OUT:O_V + (g + 1) * DIL_OUT].reshape(nb, 1, 1, HPG, HEAD_DIM)
        s_swa.append(jnp.concatenate([caches[g][:, 1:], jnp.concatenate([k_new, v_new], axis=2)], axis=1)[None])
    p_mem = mkv.reshape(batch, mt, 2, MEM_HEADS, MEM_HEAD_DIM)[None]
    s_conv = jnp.concatenate([sc0[:, 1:], zs[:, None, O_LRU_X:O_LRU_G]], axis=1)[None]
    s_rnn = h_s[None]
    return (y_prompt, y_sample, p_conv, p_rnn, p_swa[0], p_swa[1], p_swa[2], p_mem,
            s_conv, s_rnn, s_swa[0], s_swa[1], s_swa[2])
```
